```python
import jax, jax.numpy as jnp
from jax import lax
import numpy as np

D_MODEL = 2048
BATCH = 4
SEQ = 2048
DEPTH = 1
DEC_BATCH = 4
DEC_SEQ = 8192
PAST_LEN = 128

GRID_W = 64
NA_HEADS = 16
NA_WIDTH = D_MODEL // 2
NA_HEAD_DIM = NA_WIDTH // NA_HEADS
WIN_H = 8
WIN_W = 16
POOL_WINDOWS = (2, 4, 8, 16)
POOL_GROUPS = len(POOL_WINDOWS)
POOL_WIDTH = D_MODEL - NA_WIDTH
POOL_GROUP_DIM = POOL_WIDTH // POOL_GROUPS
MIX_WIDTH = NA_WIDTH + POOL_WIDTH
IN_PROJ_WIDTH = 3 * NA_WIDTH + POOL_WIDTH
N_MEM = 256
MEM_HEADS = 4
MEM_HEAD_DIM = 128
MEM_WIDTH = MEM_HEADS * MEM_HEAD_DIM
N_EXPERT_GROUPS = 4
EXPERTS_PER_GROUP = 8
N_EXPERTS = N_EXPERT_GROUPS * EXPERTS_PER_GROUP
TOP_K_FINE = 2
D_EXPERT = D_MODEL // 4
MOE_BLOCK = 256
EPS = 1e-6

kernel_name = 'hybrid_na2d_pool_memxattn_hmoe_encoder'


def rmsnorm(x, g):
    xf = x.astype(jnp.float32)
    y = xf * lax.rsqrt(jnp.mean(xf * xf, axis=-1, keepdims=True) + EPS)
    return (y * g.astype(jnp.float32)).astype(x.dtype)


def neighborhood_attention(q, k, v, rpb):
    b, seq, h, dh = q.shape
    rows = seq // GRID_W
    kh = min(WIN_H, rows)
    kw = WIN_W
    qg = q.reshape(b, rows, GRID_W, h, dh)
    kg = k.reshape(b, rows, GRID_W, h, dh)
    vg = v.reshape(b, rows, GRID_W, h, dh)
    cols = jnp.arange(GRID_W)
    col_start = jnp.clip(cols - kw // 2, 0, GRID_W - kw)
    col_idx = col_start[:, None] + jnp.arange(kw)[None, :]
    dc = col_idx - cols[:, None] + (WIN_W - 1)
    scale = dh ** -0.5

    def row_block(r):
        r0 = jnp.clip(r - kh // 2, 0, rows - kh)
        q_row = lax.dynamic_index_in_dim(qg, r, axis=1, keepdims=False)
        k_win = lax.dynamic_slice_in_dim(kg, r0, kh, axis=1)[:, :, col_idx]
        v_win = lax.dynamic_slice_in_dim(vg, r0, kh, axis=1)[:, :, col_idx]
        dr = r0 + jnp.arange(kh) - r + (WIN_H - 1)
        bias = rpb[:, dr[None, :, None], dc[:, None, :]].astype(jnp.float32)
        s = jnp.einsum('bchd,bicjhd->bhcij', q_row, k_win).astype(jnp.float32) * scale + bias[None]
        p = jax.nn.softmax(s.reshape(b, h, GRID_W, kh * kw), axis=-1).reshape(s.shape).astype(v.dtype)
        return jnp.einsum('bhcij,bicjhd->bchd', p, v_win)

    out = lax.map(row_block, jnp.arange(rows))
    return jnp.moveaxis(out, 0, 1).reshape(b, seq, h * dh)


def centred_pool_residual(u, window):
    L = u.shape[1]
    uf = u.astype(jnp.float32)
    csum = jnp.concatenate([jnp.zeros_like(uf[:, :1]), lax.cumsum(uf, axis=1)], axis=1)
    pos = jnp.arange(L)
    lo = jnp.clip(pos - window // 2, 0, L)
    hi = jnp.clip(pos - window // 2 + window, 0, L)
    mean = (csum[:, hi] - csum[:, lo]) / (hi - lo).astype(jnp.float32)[None, :, None]
    return (mean - uf).astype(u.dtype)


def multiscale_pool_mixer(u, w_pool, pool_scale):
    b, L, _ = u.shape
    ug = u.reshape(b, L, POOL_GROUPS, POOL_GROUP_DIM)
    pooled = jnp.stack([centred_pool_residual(ug[:, :, g], w) for g, w in enumerate(POOL_WINDOWS)], axis=2)
    mixed = jnp.einsum('blgc,gce->blge', pooled, w_pool)
    return mixed.reshape(b, L, POOL_WIDTH) * pool_scale


def memory_cross_attention(h, mem, g_mem, w_q, w_kv, w_o):
    b, L, _ = h.shape
    m = rmsnorm(mem, g_mem)
    n_mem = mem.shape[1]
    q = (h @ w_q).reshape(b, L, MEM_HEADS, MEM_HEAD_DIM)
    kv = (m @ w_kv).reshape(b, n_mem, 2, MEM_HEADS, MEM_HEAD_DIM)
    k, v = kv[:, :, 0], kv[:, :, 1]
    s = jnp.einsum('blhd,bmhd->bhlm', q, k).astype(jnp.float32) * (MEM_HEAD_DIM ** -0.5)
    p = jax.nn.softmax(s, axis=-1).astype(v.dtype)
    o = jnp.einsum('bhlm,bmhd->blhd', p, v).reshape(b, L, MEM_WIDTH)
    return o @ w_o


def hierarchical_moe(h, w_coarse, b_coarse, w_fine, b_fine, w_gate_e, w_up_e, w_down_e):
    b, L, d = h.shape
    xt = h.reshape(-1, d)
    T = xt.shape[0]
    coarse = jax.nn.softmax((xt @ w_coarse).astype(jnp.float32) + b_coarse.astype(jnp.float32), axis=-1)
    g_idx = jnp.argmax(coarse, axis=-1).astype(jnp.int32)
    g_gate = jnp.max(coarse, axis=-1)
    fine = ((xt @ w_fine).astype(jnp.float32) + b_fine.astype(jnp.float32)).reshape(T, N_EXPERT_GROUPS, EXPERTS_PER_GROUP)
    fine_sel = jnp.take_along_axis(fine, g_idx[:, None, None], axis=1)[:, 0]
    top_val, top_idx = lax.top_k(fine_sel, TOP_K_FINE)
    gates = g_gate[:, None] * jax.nn.softmax(top_val, axis=-1)
    expert_ids = (g_idx[:, None] * EXPERTS_PER_GROUP + top_idx.astype(jnp.int32)).reshape(-1)
    token_ids = jnp.repeat(jnp.arange(T, dtype=jnp.int32), TOP_K_FINE)
    flat_gates = gates.reshape(-1)
    n_assign = T * TOP_K_FINE
    n_blocks = -(-(n_assign + N_EXPERTS * (MOE_BLOCK - 1)) // MOE_BLOCK)
    n_slots = n_blocks * MOE_BLOCK
    order = jnp.argsort(expert_ids)
    sorted_e = expert_ids[order]
    counts = jnp.bincount(expert_ids, length=N_EXPERTS).astype(jnp.int32)
    padded = ((counts + MOE_BLOCK - 1) // MOE_BLOCK) * MOE_BLOCK
    starts = jnp.cumsum(counts) - counts
    padded_ends = jnp.cumsum(padded)
    padded_starts = padded_ends - padded
    dest = padded_starts[sorted_e] + jnp.arange(n_assign, dtype=jnp.int32) - starts[sorted_e]
    slot_tok = jnp.full((n_slots,), T, jnp.int32).at[dest].set(token_ids[order])
    slot_gate = jnp.zeros((n_slots,), jnp.float32).at[dest].set(flat_gates[order])
    block_pos = jnp.arange(n_blocks, dtype=jnp.int32) * MOE_BLOCK
    block_expert = jnp.minimum(jnp.searchsorted(padded_ends, block_pos, side='right'), N_EXPERTS - 1).astype(jnp.int32)
    x_pad = jnp.concatenate([xt, jnp.zeros((1, d), xt.dtype)], axis=0)

    def run_block(args):
        tok, e = args
        xb = x_pad[tok]
        hid = jax.nn.silu(xb @ w_gate_e[e]) * (xb @ w_up_e[e])
        return hid @ w_down_e[e]

    yb = lax.map(run_block, (slot_tok.reshape(n_blocks, MOE_BLOCK), block_expert))
    weighted = yb.reshape(n_slots, d).astype(jnp.float32) * slot_gate[:, None]
    y = jax.ops.segment_sum(weighted, slot_tok, num_segments=T + 1)[:T]
    return y.astype(h.dtype).reshape(b, L, d)


def encoder_layer(x, mem, g_mix, w_in, rpb, w_pool, pool_scale, g_branch_a, g_branch_b, w_out,
                  g_cross, g_mem, w_q_mem, w_kv_mem, w_o_mem,
                  g_moe, w_coarse, b_coarse, w_fine, b_fine, w_gate_e, w_up_e, w_down_e):
    bsz, L, _ = x.shape
    h = rmsnorm(x, g_mix)
    proj = h @ w_in
    q, k, v, u = jnp.split(proj, [NA_WIDTH, 2 * NA_WIDTH, 3 * NA_WIDTH], axis=-1)
    to_heads = lambda t: t.reshape(bsz, L, NA_HEADS, NA_HEAD_DIM)
    a_out = neighborhood_attention(to_heads(q), to_heads(k), to_heads(v), rpb)
    p_out = multiscale_pool_mixer(u, w_pool, pool_scale)
    mixed = jnp.concatenate([rmsnorm(a_out, g_branch_a), rmsnorm(p_out, g_branch_b)], axis=-1)
    x = x + mixed @ w_out
    x = x + memory_cross_attention(rmsnorm(x, g_cross), mem, g_mem, w_q_mem, w_kv_mem, w_o_mem)
    x = x + hierarchical_moe(rmsnorm(x, g_moe), w_coarse, b_coarse, w_fine, b_fine, w_gate_e, w_up_e, w_down_e)
    return x


def setup_inputs(seed: int = 0) -> dict:
    key = jax.random.key(seed)
    ks = jax.random.split(key, 32)
    f32 = jnp.float32

    def nrm(k, shape, scale):
        return jax.random.normal(k, shape, f32) * scale

    def gain(k, shape):
        return 1.0 + 0.05 * jax.random.normal(k, shape, f32)

    return {
        'x_prompt': nrm(ks[0], (BATCH, SEQ, D_MODEL), 1.0),
        'x_sample': nrm(ks[1], (DEC_BATCH, DEC_SEQ, D_MODEL), 1.0),
        'mem_prompt': nrm(ks[2], (BATCH, N_MEM, D_MODEL), 1.0),
        'mem_sample': nrm(ks[3], (DEC_BATCH, N_MEM, D_MODEL), 1.0),
        'g_mix': gain(ks[4], (DEPTH, D_MODEL)),
        'w_in': nrm(ks[5], (DEPTH, D_MODEL, IN_PROJ_WIDTH), D_MODEL ** -0.5),
        'rpb': nrm(ks[6], (DEPTH, NA_HEADS, 2 * WIN_H - 1, 2 * WIN_W - 1), 0.5),
        'w_pool': nrm(ks[7], (DEPTH, POOL_GROUPS, POOL_GROUP_DIM, POOL_GROUP_DIM), POOL_GROUP_DIM ** -0.5),
        'pool_scale': 1.0 + 0.1 * jax.random.normal(ks[8], (DEPTH, POOL_WIDTH), f32),
        'g_branch_a': gain(ks[9], (DEPTH, NA_WIDTH)),
        'g_branch_b': gain(ks[10], (DEPTH, POOL_WIDTH)),
        'w_out': nrm(ks[11], (DEPTH, MIX_WIDTH, D_MODEL), MIX_WIDTH ** -0.5),
        'g_cross': gain(ks[12], (DEPTH, D_MODEL)),
        'g_mem': gain(ks[13], (DEPTH, D_MODEL)),
        'w_q_mem': nrm(ks[14], (DEPTH, D_MODEL, MEM_WIDTH), D_MODEL ** -0.5),
        'w_kv_mem': nrm(ks[15], (DEPTH, D_MODEL, 2 * MEM_WIDTH), D_MODEL ** -0.5),
        'w_o_mem': nrm(ks[16], (DEPTH, MEM_WIDTH, D_MODEL), MEM_WIDTH ** -0.5),
        'g_moe': gain(ks[17], (DEPTH, D_MODEL)),
        'w_coarse': nrm(ks[18], (DEPTH, D_MODEL, N_EXPERT_GROUPS), D_MODEL ** -0.5),
        'b_coarse': nrm(ks[19], (DEPTH, N_EXPERT_GROUPS), 0.01),
        'w_fine': nrm(ks[20], (DEPTH, D_MODEL, N_EXPERTS), D_MODEL ** -0.5),
        'b_fine': nrm(ks[21], (DEPTH, N_EXPERTS), 0.01),
        'w_gate_e': nrm(ks[22], (DEPTH, N_EXPERTS, D_MODEL, D_EXPERT), D_MODEL ** -0.5),
        'w_up_e': nrm(ks[23], (DEPTH, N_EXPERTS, D_MODEL, D_EXPERT), D_MODEL ** -0.5),
        'w_down_e': nrm(ks[24], (DEPTH, N_EXPERTS, D_EXPERT, D_MODEL), D_EXPERT ** -0.5),
        'g_final': gain(ks[25], (D_MODEL,)),
    }


def reference(x_prompt, x_sample, mem_prompt, mem_sample, g_mix, w_in, rpb, w_pool, pool_scale,
              g_branch_a, g_branch_b, w_out, g_cross, g_mem, w_q_mem, w_kv_mem, w_o_mem,
              g_moe, w_coarse, b_coarse, w_fine, b_fine, w_gate_e, w_up_e, w_down_e, g_final):
    xp = x_prompt
    xs = x_sample
    for l in range(DEPTH):
        lp = (g_mix[l], w_in[l], rpb[l], w_pool[l], pool_scale[l], g_branch_a[l], g_branch_b[l], w_out[l],
              g_cross[l], g_mem[l], w_q_mem[l], w_kv_mem[l], w_o_mem[l],
              g_moe[l], w_coarse[l], b_coarse[l], w_fine[l], b_fine[l], w_gate_e[l], w_up_e[l], w_down_e[l])
        xp = encoder_layer(xp, mem_prompt, *lp)
        xs = encoder_layer(xs, mem_sample, *lp)
    y_prompt = rmsnorm(xp, g_final)
    y_sample = rmsnorm(xs, g_final)
    return (y_prompt, y_sample)
```

```python
import functools

import jax
import jax.numpy as jnp
from jax import lax
from jax.experimental import pallas as pl
from jax.experimental.pallas import tpu as pltpu

F32 = jnp.float32
BF16 = jnp.bfloat16

D_MODEL = 2048
GRID_W = 64
NA_HEADS = 16
NA_WIDTH = 1024
NA_HEAD_DIM = 64
WIN_H = 8
WIN_W = 16
POOL_WINDOWS = (2, 4, 8, 16)
POOL_WIDTH = 1024
POOL_GROUP_DIM = 256
POOL_HALO = 16
N_MEM = 256
MEM_HEADS = 4
MEM_HEAD_DIM = 128
MEM_WIDTH = 512
N_EXPERT_GROUPS = 4
EXPERTS_PER_GROUP = 8
N_EXPERTS = 32
D_EXPERT = 512
MOE_BLOCK = 256
EPS = 1e-6
NEG = -1e30
LANES = 128
ROW_GROUP = 8
KV_CHUNK_ROWS = 4
VMEM_LIMIT = 56 * 1024 * 1024


def _rms(x, g):
    return x * lax.rsqrt(jnp.mean(x * x, axis=-1, keepdims=True) + EPS) * g


def _const_spec(shape):
    nd = len(shape)
    return pl.BlockSpec(shape, lambda *_: (0,) * nd, pipeline_mode=pl.Buffered(1))


def _params(*sem):
    return pltpu.CompilerParams(dimension_semantics=sem, vmem_limit_bytes=VMEM_LIMIT)


def _in_proj_kernel(x_ref, g_ref, w_ref, q_ref, k_ref, v_ref, u_ref):
    h = _rms(x_ref[...], g_ref[...]).astype(BF16)
    q_ref[...] = (jnp.dot(h, w_ref[:, 0:NA_WIDTH], preferred_element_type=F32)
                  * (NA_HEAD_DIM ** -0.5)).astype(BF16)
    k_ref[...] = jnp.dot(h, w_ref[:, NA_WIDTH:2 * NA_WIDTH], preferred_element_type=F32).astype(BF16)
    v_ref[...] = jnp.dot(h, w_ref[:, 2 * NA_WIDTH:3 * NA_WIDTH], preferred_element_type=F32).astype(BF16)
    u_ref[...] = jnp.dot(h, w_ref[:, 3 * NA_WIDTH:], preferred_element_type=F32)


def _in_proj(x2d, g_mix, w_in_bf, tm=512):
    t = x2d.shape[0]
    row = lambda i: (i, 0)
    return pl.pallas_call(
        _in_proj_kernel,
        grid=(t // tm,),
        in_specs=[pl.BlockSpec((tm, D_MODEL), row), _const_spec((1, D_MODEL)),
                  _const_spec(w_in_bf.shape)],
        out_specs=[pl.BlockSpec((tm, NA_WIDTH), row)] * 4,
        out_shape=[jax.ShapeDtypeStruct((t, NA_WIDTH), BF16)] * 3
        + [jax.ShapeDtypeStruct((t, POOL_WIDTH), F32)],
        compiler_params=_params("parallel"),
        name="in_proj",
    )(x2d, g_mix, w_in_bf)


def _norm_matmul_kernel(x_ref, g_ref, w_ref, o_ref):
    h = _rms(x_ref[...], g_ref[...]).astype(BF16)
    o_ref[...] = jnp.dot(h, w_ref[...], preferred_element_type=F32).astype(o_ref.dtype)


def _norm_matmul(x2d, g, w_bf, out_dtype, tm=256):
    t, k = x2d.shape
    n = w_bf.shape[1]
    return pl.pallas_call(
        _norm_matmul_kernel,
        grid=(t // tm,),
        in_specs=[pl.BlockSpec((tm, k), lambda i: (i, 0)), _const_spec((1, k)), _const_spec((k, n))],
        out_specs=pl.BlockSpec((tm, n), lambda i: (i, 0)),
        out_shape=jax.ShapeDtypeStruct((t, n), out_dtype),
        compiler_params=_params("parallel"),
        name="norm_matmul",
    )(x2d, g, w_bf)


def _pool_kernel(prev_ref, cur_ref, next_ref, w_ref, scale_ref, g_ref, o_ref, ext_ref, mix_ref, *, seq_len, tm):
    i = pl.program_id(1)
    n_i = pl.num_programs(1)
    ext_ref[0:POOL_HALO, :] = jnp.where(i > 0, prev_ref[0], 0.0)
    ext_ref[POOL_HALO:POOL_HALO + tm, :] = cur_ref[0]
    ext_ref[POOL_HALO + tm:, :] = jnp.where(i < n_i - 1, next_ref[0], 0.0)
    n_ext = tm + 2 * POOL_HALO
    pos = i * tm + lax.broadcasted_iota(jnp.int32, (tm, 1), 0)
    for gi, w in enumerate(POOL_WINDOWS):
        cols = slice(gi * POOL_GROUP_DIM, (gi + 1) * POOL_GROUP_DIM)
        e = ext_ref[:, cols]
        s = e + pltpu.roll(e, 1, 0)
        half = 1
        while 2 * half < w:
            s = pltpu.roll(s, half, 0) + pltpu.roll(s, n_ext - half, 0)
            half *= 2
        lo = jnp.maximum(pos - w // 2, 0)
        hi = jnp.minimum(pos - w // 2 + w, seq_len)
        cnt = (hi - lo).astype(F32)
        pooled = s[POOL_HALO:POOL_HALO + tm] / cnt - e[POOL_HALO:POOL_HALO + tm]
        mix_ref[:, cols] = jnp.dot(pooled.astype(BF16), w_ref[gi], preferred_element_type=F32)
    mixed = mix_ref[...] * scale_ref[...]
    o_ref[0] = _rms(mixed, g_ref[...]).astype(BF16)


def _pool_mixer(u3d, w_pool_bf, pool_scale, g_b, tm=512):
    b, seq_len, _ = u3d.shape
    tm = min(tm, seq_len)
    hb = tm // POOL_HALO
    n_hb = seq_len // POOL_HALO
    kern = functools.partial(_pool_kernel, seq_len=seq_len, tm=tm)
    return pl.pallas_call(
        kern,
        grid=(b, seq_len // tm),
        in_specs=[
            pl.BlockSpec((1, POOL_HALO, POOL_WIDTH), lambda bi, i: (bi, jnp.maximum(i * hb - 1, 0), 0)),
            pl.BlockSpec((1, tm, POOL_WIDTH), lambda bi, i: (bi, i, 0)),
            pl.BlockSpec((1, POOL_HALO, POOL_WIDTH), lambda bi, i: (bi, jnp.minimum((i + 1) * hb, n_hb - 1), 0)),
            _const_spec(w_pool_bf.shape), _const_spec((1, POOL_WIDTH)), _const_spec((1, POOL_WIDTH)),
        ],
        out_specs=pl.BlockSpec((1, tm, POOL_WIDTH), lambda bi, i: (bi, i, 0)),
        out_shape=jax.ShapeDtypeStruct((b, seq_len, POOL_WIDTH), BF16),
        scratch_shapes=[pltpu.VMEM((tm + 2 * POOL_HALO, POOL_WIDTH), F32), pltpu.VMEM((tm, POOL_WIDTH), F32)],
        compiler_params=_params("parallel", "parallel"),
        name="pool_mixer",
    )(u3d, u3d, u3d, w_pool_bf, pool_scale, g_b)


def _na_bias_table(rpb):
    pat = jnp.arange(WIN_H)[:, None]
    krow = jnp.arange(WIN_H)[None, :]
    dr = krow - pat + (WIN_H - 1)
    c = jnp.arange(GRID_W)[:, None]
    kc = jnp.arange(GRID_W)[None, :]
    start = jnp.clip(c - WIN_W // 2, 0, GRID_W - WIN_W)
    valid = (kc >= start) & (kc < start + WIN_W)
    dc = jnp.clip(kc - c + (WIN_W - 1), 0, 2 * WIN_W - 2)
    tbl = rpb.astype(F32)[:, dr[:, :, None, None], dc[None, None, :, :]]
    tbl = jnp.where(valid[None, None, None], tbl, NEG)
    tbl = jnp.transpose(tbl, (1, 0, 3, 2, 4))
    return tbl.reshape(WIN_H, NA_HEADS // 2, 2 * GRID_W, WIN_H * GRID_W)


def _na_kernel(q_ref, k0, k1, k2, k3, v0, v1, v2, v3, bias_ref, g_ref, o_ref, kbuf, vbuf, acc, *, n_rows):
    g = pl.program_id(1)
    chunk = KV_CHUNK_ROWS * GRID_W
    for c, (kr, vr) in enumerate(((k0, v0), (k1, v1), (k2, v2), (k3, v3))):
        kbuf[c * chunk:(c + 1) * chunk, :] = kr[0]
        vbuf[c * chunk:(c + 1) * chunk, :] = vr[0]
    first_head = lax.broadcasted_iota(jnp.int32, (GRID_W, LANES), 1) < NA_HEAD_DIM
    n_keys = WIN_H * GRID_W

    def row_body(j, carry):
        r = g * ROW_GROUP + j
        r0 = jnp.clip(r - WIN_H // 2, 0, n_rows - WIN_H)
        pat = r - r0
        koff = pl.multiple_of((r0 - (g * ROW_GROUP - KV_CHUNK_ROWS)) * GRID_W, GRID_W)
        qoff = pl.multiple_of(j * GRID_W, GRID_W)
        for hp in range(NA_HEADS // 2):
            cols = slice(hp * LANES, (hp + 1) * LANES)
            q2 = q_ref[0, pl.ds(qoff, GRID_W), cols]
            zero = jnp.zeros_like(q2)
            qq = jnp.concatenate([jnp.where(first_head, q2, zero), jnp.where(first_head, zero, q2)], axis=0)
            kk = kbuf[pl.ds(koff, n_keys), cols]
            vv = vbuf[pl.ds(koff, n_keys), cols]
            s = lax.dot_general(qq, kk, (((1,), (1,)), ((), ())), preferred_element_type=F32)
            s = s + bias_ref[pat, hp]
            m = jnp.max(s, axis=-1, keepdims=True)
            p = jnp.exp(s - m)
            l = jnp.sum(p, axis=-1, keepdims=True)
            o = jnp.dot(p.astype(BF16), vv, preferred_element_type=F32) / l
            acc[pl.ds(qoff, GRID_W), cols] = jnp.where(first_head, o[:GRID_W], o[GRID_W:])
        return carry

    lax.fori_loop(0, ROW_GROUP, row_body, 0)
    o_ref[0] = _rms(acc[...], g_ref[...]).astype(BF16)


def _neighborhood_attention(q3d, k3d, v3d, bias_tbl, g_a):
    b, seq_len, _ = q3d.shape
    n_rows = seq_len // GRID_W
    tq = ROW_GROUP * GRID_W
    chunk = KV_CHUNK_ROWS * GRID_W
    n_chunks = seq_len // chunk

    def kv_spec(c):
        return pl.BlockSpec((1, chunk, NA_WIDTH),
                            lambda bi, g: (bi, jnp.clip(2 * g - 1 + c, 0, n_chunks - 1), 0))

    kern = functools.partial(_na_kernel, n_rows=n_rows)
    return pl.pallas_call(
        kern,
        grid=(b, n_rows // ROW_GROUP),
        in_specs=[pl.BlockSpec((1, tq, NA_WIDTH), lambda bi, g: (bi, g, 0))]
        + [kv_spec(c) for c in range(4)] * 2
        + [_const_spec(bias_tbl.shape), _const_spec((1, NA_WIDTH))],
        out_specs=pl.BlockSpec((1, tq, NA_WIDTH), lambda bi, g: (bi, g, 0)),
        out_shape=jax.ShapeDtypeStruct((b, seq_len, NA_WIDTH), BF16),
        scratch_shapes=[pltpu.VMEM((4 * chunk, NA_WIDTH), BF16), pltpu.VMEM((4 * chunk, NA_WIDTH), BF16),
                        pltpu.VMEM((tq, NA_WIDTH), F32)],
        compiler_params=_params("parallel", "parallel"),
        name="neighborhood_attention",
    )(q3d, k3d, k3d, k3d, k3d, v3d, v3d, v3d, v3d, bias_tbl, g_a)


def _post_kernel(x_ref, a_ref, p_ref, wout_ref, gc_ref, wq_ref, km_ref, vm_ref, wo_ref, gm_ref,
                 wr_hi_ref, wr_lo_ref, br_ref, x2_ref, h3_ref, eid_ref, gate_ref):
    tm = x_ref.shape[1]
    x1 = (x_ref[0]
          + jnp.dot(a_ref[0], wout_ref[0:NA_WIDTH, :], preferred_element_type=F32)
          + jnp.dot(p_ref[0], wout_ref[NA_WIDTH:, :], preferred_element_type=F32))
    h2 = _rms(x1, gc_ref[...]).astype(BF16)
    qm = jnp.dot(h2, wq_ref[...], preferred_element_type=F32).astype(BF16)
    heads = []
    for hd in range(MEM_HEADS):
        cols = slice(hd * MEM_HEAD_DIM, (hd + 1) * MEM_HEAD_DIM)
        s = lax.dot_general(qm[:, cols], km_ref[0, :, cols], (((1,), (1,)), ((), ())),
                            preferred_element_type=F32) * (MEM_HEAD_DIM ** -0.5)
        m = jnp.max(s, axis=-1, keepdims=True)
        p = jnp.exp(s - m)
        l = jnp.sum(p, axis=-1, keepdims=True)
        o = jnp.dot(p.astype(BF16), vm_ref[0, :, cols], preferred_element_type=F32) / l
        heads.append(o.astype(BF16))
    o_all = jnp.concatenate(heads, axis=-1)
    x2 = x1 + jnp.dot(o_all, wo_ref[...], preferred_element_type=F32)
    x2_ref[0] = x2
    h3 = _rms(x2, gm_ref[...])
    h3_ref[0] = h3

    h_hi = h3.astype(BF16)
    h_lo = (h3 - h_hi.astype(F32)).astype(BF16)
    logits = (jnp.dot(h_hi, wr_hi_ref[...], preferred_element_type=F32)
              + jnp.dot(h_lo, wr_hi_ref[...], preferred_element_type=F32)
              + jnp.dot(h_hi, wr_lo_ref[...], preferred_element_type=F32)) + br_ref[...]

    lane = lax.broadcasted_iota(jnp.int32, (tm, LANES), 1)
    lane_f = lane.astype(F32)
    big = float(LANES)
    cmask = lane < N_EXPERT_GROUPS
    lc = jnp.where(cmask, logits, NEG)
    cmax = jnp.max(lc, axis=-1, keepdims=True)
    g_idx = jnp.min(jnp.where(lc == cmax, lane_f, big), axis=-1, keepdims=True)
    g_gate = 1.0 / jnp.sum(jnp.exp(lc - cmax), axis=-1, keepdims=True)
    grp_f = ((lane - N_EXPERT_GROUPS) >> 3).astype(F32)
    in_fine = jnp.where(lane >= N_EXPERT_GROUPS, jnp.where(lane < N_EXPERT_GROUPS + N_EXPERTS, grp_f, -1.0), -1.0)
    lf = jnp.where(in_fine == g_idx, logits, NEG)
    v1 = jnp.max(lf, axis=-1, keepdims=True)
    i1 = jnp.min(jnp.where(lf == v1, lane_f, big), axis=-1, keepdims=True)
    lf2 = jnp.where(lane_f == i1, NEG, lf)
    v2 = jnp.max(lf2, axis=-1, keepdims=True)
    i2 = jnp.min(jnp.where(lf2 == v2, lane_f, big), axis=-1, keepdims=True)
    e21 = jnp.exp(v2 - v1)
    den = 1.0 / (1.0 + e21)
    two = lax.broadcasted_iota(jnp.int32, (tm, 2), 1)
    eid_ref[0] = jnp.where(two == 0, i1, i2).astype(jnp.int32) - N_EXPERT_GROUPS
    gate_ref[0] = jnp.where(two == 0, g_gate * den, g_gate * (e21 * den))


def _post_mix(x3d, a_n, p_n, w_out_bf, g_cross, w_q_bf, k_mem, v_mem, w_o_bf, g_moe, wr_hi, wr_lo, b_r, tm=256):
    b, seq_len, _ = x3d.shape
    tok = lambda w: pl.BlockSpec((1, tm, w), lambda bi, i: (bi, i, 0))
    mem = pl.BlockSpec((1, N_MEM, MEM_WIDTH), lambda bi, i: (bi, 0, 0))
    return pl.pallas_call(
        _post_kernel,
        grid=(b, seq_len // tm),
        in_specs=[tok(D_MODEL), tok(NA_WIDTH), tok(POOL_WIDTH), _const_spec(w_out_bf.shape),
                  _const_spec((1, D_MODEL)), _const_spec(w_q_bf.shape), mem, mem, _const_spec(w_o_bf.shape),
                  _const_spec((1, D_MODEL)), _const_spec(wr_hi.shape), _const_spec(wr_lo.shape),
                  _const_spec((1, LANES))],
        out_specs=[tok(D_MODEL), tok(D_MODEL), tok(2), tok(2)],
        out_shape=[jax.ShapeDtypeStruct((b, seq_len, D_MODEL), F32), jax.ShapeDtypeStruct((b, seq_len, D_MODEL), F32),
                   jax.ShapeDtypeStruct((b, seq_len, 2), jnp.int32), jax.ShapeDtypeStruct((b, seq_len, 2), F32)],
        compiler_params=_params("parallel", "parallel"),
        name="post_mix",
    )(x3d, a_n, p_n, w_out_bf, g_cross, w_q_bf, k_mem, v_mem, w_o_bf, g_moe, wr_hi, wr_lo, b_r)


def _rank_kernel(eid_ref, rank_ref, cnt_ref, carry):
    tm = eid_ref.shape[0]

    @pl.when(pl.program_id(0) == 0)
    def _():
        carry[...] = jnp.zeros_like(carry)

    lane = lax.broadcasted_iota(jnp.int32, (tm, LANES), 1)
    e1 = eid_ref[:, 0:1]
    e2 = eid_ref[:, 1:2]
    hit1 = lane == e1
    hit2 = lane == e2
    onehot = jnp.where(hit1, 1.0, 0.0) + jnp.where(hit2, 1.0, 0.0)
    row = lax.broadcasted_iota(jnp.int32, (tm, tm), 0)
    col = lax.broadcasted_iota(jnp.int32, (tm, tm), 1)
    earlier = jnp.where(col < row, 1.0, 0.0).astype(BF16)
    before = jnp.dot(earlier, onehot.astype(BF16), preferred_element_type=F32) + carry[...]
    r1 = jnp.sum(jnp.where(hit1, before, 0.0), axis=-1, keepdims=True)
    r2 = jnp.sum(jnp.where(hit2, before, 0.0), axis=-1, keepdims=True)
    two = lax.broadcasted_iota(jnp.int32, (tm, 2), 1)
    rank_ref[...] = jnp.where(two == 0, r1, r2).astype(jnp.int32)
    total = carry[...] + jnp.sum(onehot, axis=0, keepdims=True)
    carry[...] = total
    cnt_ref[...] = total.astype(jnp.int32)


def _expert_ranks(eid2d, tm=512):
    t = eid2d.shape[0]
    return pl.pallas_call(
        _rank_kernel,
        grid=(t // tm,),
        in_specs=[pl.BlockSpec((tm, 2), lambda i: (i, 0))],
        out_specs=[pl.BlockSpec((tm, 2), lambda i: (i, 0)), pl.BlockSpec((1, LANES), lambda i: (0, 0))],
        out_shape=[jax.ShapeDtypeStruct((t, 2), jnp.int32), jax.ShapeDtypeStruct((1, LANES), jnp.int32)],
        scratch_shapes=[pltpu.VMEM((1, LANES), F32)],
        compiler_params=_params("arbitrary"),
        name="expert_ranks",
    )(eid2d)


def _dispatch_kernel(dest_ref, h_ref, xs_in_ref, xs_ref, sem, *, tm):
    del xs_in_ref
    base = pl.program_id(0) * tm

    def issue(n, carry):
        t = base + n
        for k in range(2):
            pltpu.make_async_copy(h_ref.at[pl.ds(t, 1)], xs_ref.at[pl.ds(dest_ref[2 * t + k], 1)], sem).start()
        return carry

    lax.fori_loop(0, tm, issue, 0)

    def drain(n, carry):
        for k in range(2):
            pltpu.make_async_copy(h_ref.at[pl.ds(0, 1)], xs_ref.at[pl.ds(0, 1)], sem).wait()
        return carry

    lax.fori_loop(0, tm, drain, 0)


def _dispatch(dest_flat, h2d, n_slots, tm=512):
    t, d = h2d.shape
    xs0 = jnp.zeros((n_slots, d), h2d.dtype)
    kern = functools.partial(_dispatch_kernel, tm=tm)
    return pl.pallas_call(
        kern,
        grid_spec=pltpu.PrefetchScalarGridSpec(
            num_scalar_prefetch=1,
            grid=(t // tm,),
            in_specs=[pl.BlockSpec(memory_space=pl.ANY), pl.BlockSpec(memory_space=pl.ANY)],
            out_specs=pl.BlockSpec(memory_space=pl.ANY),
            scratch_shapes=[pltpu.SemaphoreType.DMA],
        ),
        out_shape=jax.ShapeDtypeStruct((n_slots, d), h2d.dtype),
        input_output_aliases={2: 0},
        compiler_params=pltpu.CompilerParams(dimension_semantics=("arbitrary",), has_side_effects=True),
        name="moe_dispatch",
    )(dest_flat, h2d, xs0)


def _expert_kernel(be_ref, nused_ref, xs_ref, wgu_ref, wd_ref, ys_ref):
    i = pl.program_id(0)

    @pl.when(i < nused_ref[0])
    def _():
        x = xs_ref[...].astype(BF16)
        gu = jnp.dot(x, wgu_ref[0], preferred_element_type=F32)
        gate = gu[:, :D_EXPERT]
        hid = gate * jax.nn.sigmoid(gate) * gu[:, D_EXPERT:]
        ys_ref[...] = jnp.dot(hid.astype(BF16), wd_ref[0], preferred_element_type=F32)

    @pl.when(i >= nused_ref[0])
    def _():
        ys_ref[...] = jnp.zeros_like(ys_ref)


def _experts(block_expert, n_used, xs, w_gu_bf, w_d_bf):
    n_slots, d = xs.shape
    n_blocks = n_slots // MOE_BLOCK
    return pl.pallas_call(
        _expert_kernel,
        grid_spec=pltpu.PrefetchScalarGridSpec(
            num_scalar_prefetch=2,
            grid=(n_blocks,),
            in_specs=[pl.BlockSpec((MOE_BLOCK, d), lambda i, be, nu: (i, 0)),
                      pl.BlockSpec((1, d, 2 * D_EXPERT), lambda i, be, nu: (be[i], 0, 0)),
                      pl.BlockSpec((1, D_EXPERT, d), lambda i, be, nu: (be[i], 0, 0))],
            out_specs=pl.BlockSpec((MOE_BLOCK, d), lambda i, be, nu: (i, 0)),
        ),
        out_shape=jax.ShapeDtypeStruct((n_slots, d), F32),
        compiler_params=_params("arbitrary"),
        name="moe_experts",
    )(block_expert, n_used, xs, w_gu_bf, w_d_bf)


def _combine_kernel(dest_ref, x2_ref, gate_ref, g_ref, ys_ref, o_ref, ybuf, sem, *, tm):
    base = pl.program_id(0) * tm

    def issue(n, carry):
        t = base + n
        for k in range(2):
            pltpu.make_async_copy(ys_ref.at[pl.ds(dest_ref[2 * t + k], 1)], ybuf.at[k, pl.ds(n, 1)], sem).start()
        return carry

    lax.fori_loop(0, tm, issue, 0)

    def drain(n, carry):
        for k in range(2):
            pltpu.make_async_copy(ys_ref.at[pl.ds(0, 1)], ybuf.at[k, pl.ds(0, 1)], sem).wait()
        return carry

    lax.fori_loop(0, tm, drain, 0)
    gates = gate_ref[...]
    y = gates[:, 0:1] * ybuf[0] + gates[:, 1:2] * ybuf[1]
    o_ref[...] = _rms(x2_ref[...] + y, g_ref[...])


def _combine(dest_flat, x2_2d, gate2d, g_final, ys, tm=256):
    t, d = x2_2d.shape
    kern = functools.partial(_combine_kernel, tm=tm)
    return pl.pallas_call(
        kern,
        grid_spec=pltpu.PrefetchScalarGridSpec(
            num_scalar_prefetch=1,
            grid=(t // tm,),
            in_specs=[pl.BlockSpec((tm, d), lambda i, dst: (i, 0)),
                      pl.BlockSpec((tm, 2), lambda i, dst: (i, 0)),
                      pl.BlockSpec((1, d), lambda i, dst: (0, 0)),
                      pl.BlockSpec(memory_space=pl.ANY)],
            out_specs=pl.BlockSpec((tm, d), lambda i, dst: (i, 0)),
            scratch_shapes=[pltpu.VMEM((2, tm, d), F32), pltpu.SemaphoreType.DMA],
        ),
        out_shape=jax.ShapeDtypeStruct((t, d), F32),
        compiler_params=_params("arbitrary"),
        name="moe_combine",
    )(dest_flat, x2_2d, gate2d, g_final, ys)


def _encode(x, mem, p):
    b, seq_len, d = x.shape
    t = b * seq_len
    q, k, v, u = _in_proj(x.reshape(t, d), p["g_mix"], p["w_in"])
    shape3 = (b, seq_len, NA_WIDTH)
    a_n = _neighborhood_attention(q.reshape(shape3), k.reshape(shape3), v.reshape(shape3), p["na_bias"], p["g_branch_a"])
    p_n = _pool_mixer(u.reshape(shape3), p["w_pool"], p["pool_scale"], p["g_branch_b"])
    kv = _norm_matmul(mem.reshape(b * N_MEM, d), p["g_mem"], p["w_kv"], BF16).reshape(b, N_MEM, 2 * MEM_WIDTH)
    x2, h3, eid, gate = _post_mix(x, a_n, p_n, p["w_out"], p["g_cross"], p["w_q"], kv[:, :, :MEM_WIDTH],
                                  kv[:, :, MEM_WIDTH:], p["w_o"], p["g_moe"], p["wr_hi"], p["wr_lo"], p["b_r"])
    eid2d = eid.reshape(t, 2)
    rank, counts = _expert_ranks(eid2d)
    counts = counts[0, :N_EXPERTS]
    padded = ((counts + MOE_BLOCK - 1) // MOE_BLOCK) * MOE_BLOCK
    padded_ends = jnp.cumsum(padded)
    padded_starts = padded_ends - padded
    n_blocks = -(-(2 * t + N_EXPERTS * (MOE_BLOCK - 1)) // MOE_BLOCK)
    block_pos = jnp.arange(n_blocks, dtype=jnp.int32) * MOE_BLOCK
    block_expert = jnp.minimum(jnp.sum(padded_ends[None, :] <= block_pos[:, None], axis=1), N_EXPERTS - 1).astype(jnp.int32)
    n_used = (padded_ends[-1:] // MOE_BLOCK).astype(jnp.int32)
    dest = (padded_starts[eid2d] + rank).astype(jnp.int32).reshape(-1)
    xs = _dispatch(dest, h3.reshape(t, d), n_blocks * MOE_BLOCK)
    ys = _experts(block_expert, n_used, xs, p["w_gu"], p["w_d"])
    y = _combine(dest, x2.reshape(t, d), gate.reshape(t, 2), p["g_final"], ys)
    return y.reshape(b, seq_len, d)


def kernel(x_prompt, x_sample, mem_prompt, mem_sample, g_mix, w_in, rpb, w_pool, pool_scale, g_branch_a, g_branch_b, w_out, g_cross, g_mem, w_q_mem, w_kv_mem, w_o_mem, g_moe, w_coarse, b_coarse, w_fine, b_fine, w_gate_e, w_up_e, w_down_e, g_final):
    depth = w_in.shape[0]
    xp, xs = x_prompt, x_sample
    row = lambda a: a.reshape(1, -1).astype(F32)
    for l in range(depth):
        w_r = jnp.concatenate([w_coarse[l], w_fine[l]], axis=1).astype(F32)
        w_r = jnp.pad(w_r, ((0, 0), (0, LANES - w_r.shape[1])))
        wr_hi = w_r.astype(BF16)
        b_r = jnp.concatenate([b_coarse[l], b_fine[l]]).astype(F32)
        p = {
            "g_mix": row(g_mix[l]), "w_in": w_in[l].astype(BF16),
            "na_bias": _na_bias_table(rpb[l]), "g_branch_a": row(g_branch_a[l]),
            "w_pool": w_pool[l].astype(BF16), "pool_scale": row(pool_scale[l]), "g_branch_b": row(g_branch_b[l]),
            "w_out": w_out[l].astype(BF16), "g_cross": row(g_cross[l]), "g_mem": row(g_mem[l]),
            "w_q": w_q_mem[l].astype(BF16), "w_kv": w_kv_mem[l].astype(BF16), "w_o": w_o_mem[l].astype(BF16),
            "g_moe": row(g_moe[l]),
            "wr_hi": wr_hi, "wr_lo": (w_r - wr_hi.astype(F32)).astype(BF16),
            "b_r": jnp.pad(b_r, (0, LANES - b_r.shape[0])).reshape(1, LANES),
            "w_gu": jnp.concatenate([w_gate_e[l], w_up_e[l]], axis=-1).astype(BF16),
            "w_d": w_down_e[l].astype(BF16),
            "g_final": row(g_final),
        }
        assert depth == 1, "final-norm fusion assumes a single layer"
        xp = _encode(xp, mem_prompt, p)
        xs = _encode(xs, mem_sample, p)
    return (xp, xs)
```

```python
import functools

import jax
import jax.numpy as jnp
from jax import lax
from jax.experimental import pallas as pl
from jax.experimental.pallas import tpu as pltpu

F32 = jnp.float32
BF16 = jnp.bfloat16

D_MODEL = 2048
GRID_W = 64
NA_HEADS = 16
NA_WIDTH = 1024
NA_HEAD_DIM = 64
WIN_H = 8
WIN_W = 16
POOL_WINDOWS = (2, 4, 8, 16)
POOL_WIDTH = 1024
POOL_GROUP_DIM = 256
POOL_HALO = 16
N_MEM = 256
MEM_HEADS = 4
MEM_HEAD_DIM = 128
MEM_WIDTH = 512
N_EXPERT_GROUPS = 4
EXPERTS_PER_GROUP = 8
N_EXPERTS = 32
D_EXPERT = 512
MOE_BLOCK = 256
EPS = 1e-6
NEG = -1e30
LANES = 128
ROW_GROUP = 8
KV_CHUNK_ROWS = 4
VMEM_LIMIT = 56 * 1024 * 1024


def _rms(x, g):
    return x * lax.rsqrt(jnp.mean(x * x, axis=-1, keepdims=True) + EPS) * g


def _const_spec(shape):
    nd = len(shape)
    return pl.BlockSpec(shape, lambda *_: (0,) * nd, pipeline_mode=pl.Buffered(1))


def _params(*sem):
    return pltpu.CompilerParams(dimension_semantics=sem, vmem_limit_bytes=VMEM_LIMIT)


def _in_proj_kernel(x_ref, g_ref, w_ref, q_ref, k_ref, v_ref, u_ref):
    h = _rms(x_ref[...], g_ref[...]).astype(BF16)
    q_ref[...] = (jnp.dot(h, w_ref[:, 0:NA_WIDTH], preferred_element_type=F32)
                  * (NA_HEAD_DIM ** -0.5)).astype(BF16)
    k_ref[...] = jnp.dot(h, w_ref[:, NA_WIDTH:2 * NA_WIDTH], preferred_element_type=F32).astype(BF16)
    v_ref[...] = jnp.dot(h, w_ref[:, 2 * NA_WIDTH:3 * NA_WIDTH], preferred_element_type=F32).astype(BF16)
    u_ref[...] = jnp.dot(h, w_ref[:, 3 * NA_WIDTH:], preferred_element_type=F32)


def _in_proj(x2d, g_mix, w_in_bf, tm=512):
    t = x2d.shape[0]
    row = lambda i: (i, 0)
    return pl.pallas_call(
        _in_proj_kernel,
        grid=(t // tm,),
        in_specs=[pl.BlockSpec((tm, D_MODEL), row), _const_spec((1, D_MODEL)),
                  _const_spec(w_in_bf.shape)],
        out_specs=[pl.BlockSpec((tm, NA_WIDTH), row)] * 4,
        out_shape=[jax.ShapeDtypeStruct((t, NA_WIDTH), BF16)] * 3
        + [jax.ShapeDtypeStruct((t, POOL_WIDTH), F32)],
        compiler_params=_params("parallel"),
        name="in_proj",
    )(x2d, g_mix, w_in_bf)


def _norm_matmul_kernel(x_ref, g_ref, w_ref, o_ref):
    h = _rms(x_ref[...], g_ref[...]).astype(BF16)
    o_ref[...] = jnp.dot(h, w_ref[...], preferred_element_type=F32).astype(o_ref.dtype)


def _norm_matmul(x2d, g, w_bf, out_dtype, tm=256):
    t, k = x2d.shape
    n = w_bf.shape[1]
    return pl.pallas_call(
        _norm_matmul_kernel,
        grid=(t // tm,),
        in_specs=[pl.BlockSpec((tm, k), lambda i: (i, 0)), _const_spec((1, k)), _const_spec((k, n))],
        out_specs=pl.BlockSpec((tm, n), lambda i: (i, 0)),
        out_shape=jax.ShapeDtypeStruct((t, n), out_dtype),
        compiler_params=_params("parallel"),
        name="norm_matmul",
    )(x2d, g, w_bf)


def _pool_kernel(prev_ref, cur_ref, next_ref, w_ref, scale_ref, g_ref, o_ref, ext_ref, mix_ref, *, seq_len, tm):
    i = pl.program_id(1)
    n_i = pl.num_programs(1)
    ext_ref[0:POOL_HALO, :] = jnp.where(i > 0, prev_ref[0], 0.0)
    ext_ref[POOL_HALO:POOL_HALO + tm, :] = cur_ref[0]
    ext_ref[POOL_HALO + tm:, :] = jnp.where(i < n_i - 1, next_ref[0], 0.0)
    n_ext = tm + 2 * POOL_HALO
    pos = i * tm + lax.broadcasted_iota(jnp.int32, (tm, 1), 0)
    for gi, w in enumerate(POOL_WINDOWS):
        cols = slice(gi * POOL_GROUP_DIM, (gi + 1) * POOL_GROUP_DIM)
        e = ext_ref[:, cols]
        s = e + pltpu.roll(e, 1, 0)
        half = 1
        while 2 * half < w:
            s = pltpu.roll(s, half, 0) + pltpu.roll(s, n_ext - half, 0)
            half *= 2
        lo = jnp.maximum(pos - w // 2, 0)
        hi = jnp.minimum(pos - w // 2 + w, seq_len)
        cnt = (hi - lo).astype(F32)
        pooled = s[POOL_HALO:POOL_HALO + tm] / cnt - e[POOL_HALO:POOL_HALO + tm]
        mix_ref[:, cols] = jnp.dot(pooled.astype(BF16), w_ref[gi], preferred_element_type=F32)
    mixed = mix_ref[...] * scale_ref[...]
    o_ref[0] = _rms(mixed, g_ref[...]).astype(BF16)


def _pool_mixer(u3d, w_pool_bf, pool_scale, g_b, tm=512):
    b, seq_len, _ = u3d.shape
    tm = min(tm, seq_len)
    hb = tm // POOL_HALO
    n_hb = seq_len // POOL_HALO
    kern = functools.partial(_pool_kernel, seq_len=seq_len, tm=tm)
    return pl.pallas_call(
        kern,
        grid=(b, seq_len // tm),
        in_specs=[
            pl.BlockSpec((1, POOL_HALO, POOL_WIDTH), lambda bi, i: (bi, jnp.maximum(i * hb - 1, 0), 0)),
            pl.BlockSpec((1, tm, POOL_WIDTH), lambda bi, i: (bi, i, 0)),
            pl.BlockSpec((1, POOL_HALO, POOL_WIDTH), lambda bi, i: (bi, jnp.minimum((i + 1) * hb, n_hb - 1), 0)),
            _const_spec(w_pool_bf.shape), _const_spec((1, POOL_WIDTH)), _const_spec((1, POOL_WIDTH)),
        ],
        out_specs=pl.BlockSpec((1, tm, POOL_WIDTH), lambda bi, i: (bi, i, 0)),
        out_shape=jax.ShapeDtypeStruct((b, seq_len, POOL_WIDTH), BF16),
        scratch_shapes=[pltpu.VMEM((tm + 2 * POOL_HALO, POOL_WIDTH), F32), pltpu.VMEM((tm, POOL_WIDTH), F32)],
        compiler_params=_params("parallel", "parallel"),
        name="pool_mixer",
    )(u3d, u3d, u3d, w_pool_bf, pool_scale, g_b)


def _na_bias_table(rpb):
    c = jnp.arange(GRID_W)[:, None]
    kc = jnp.arange(GRID_W)[None, :]
    start = jnp.clip(c - WIN_W // 2, 0, GRID_W - WIN_W)
    valid = (kc >= start) & (kc < start + WIN_W)
    dc = kc - c + (WIN_W - 1)
    onehot = (dc[:, :, None] == jnp.arange(2 * WIN_W - 1)[None, None, :]).astype(F32)
    band = jnp.einsum("hrd,cqd->hrcq", rpb.astype(F32), onehot, precision=lax.Precision.HIGHEST)
    band = jnp.where(valid[None, None], band, NEG)
    pats = []
    for pat in range(WIN_H):
        win = band[:, WIN_H - 1 - pat:2 * WIN_H - 1 - pat]
        pats.append(jnp.transpose(win, (0, 2, 1, 3)).reshape(NA_HEADS // 2, 2 * GRID_W, WIN_H * GRID_W))
    return jnp.stack(pats)


def _na_kernel(q_ref, k0, k1, k2, k3, v0, v1, v2, v3, bias_ref, g_ref, o_ref, kbuf, vbuf, acc, *, n_rows):
    g = pl.program_id(1)
    chunk = KV_CHUNK_ROWS * GRID_W
    for c, (kr, vr) in enumerate(((k0, v0), (k1, v1), (k2, v2), (k3, v3))):
        kbuf[c * chunk:(c + 1) * chunk, :] = kr[0]
        vbuf[c * chunk:(c + 1) * chunk, :] = vr[0]
    first_head = lax.broadcasted_iota(jnp.int32, (GRID_W, LANES), 1) < NA_HEAD_DIM
    n_keys = WIN_H * GRID_W

    def row_body(j, carry):
        r = g * ROW_GROUP + j
        r0 = jnp.clip(r - WIN_H // 2, 0, n_rows - WIN_H)
        pat = r - r0
        koff = pl.multiple_of((r0 - (g * ROW_GROUP - KV_CHUNK_ROWS)) * GRID_W, GRID_W)
        qoff = pl.multiple_of(j * GRID_W, GRID_W)
        for hp in range(NA_HEADS // 2):
            cols = slice(hp * LANES, (hp + 1) * LANES)
            q2 = q_ref[0, pl.ds(qoff, GRID_W), cols]
            zero = jnp.zeros_like(q2)
            qq = jnp.concatenate([jnp.where(first_head, q2, zero), jnp.where(first_head, zero, q2)], axis=0)
            kk = kbuf[pl.ds(koff, n_keys), cols]
            vv = vbuf[pl.ds(koff, n_keys), cols]
            s = lax.dot_general(qq, kk, (((1,), (1,)), ((), ())), preferred_element_type=F32)
            s = s + bias_ref[pat, hp]
            m = jnp.max(s, axis=-1, keepdims=True)
            p = jnp.exp(s - m)
            l = jnp.sum(p, axis=-1, keepdims=True)
            o = jnp.dot(p.astype(BF16), vv, preferred_element_type=F32) / l
            acc[pl.ds(qoff, GRID_W), cols] = jnp.where(first_head, o[:GRID_W], o[GRID_W:])
        return carry

    lax.fori_loop(0, ROW_GROUP, row_body, 0)
    o_ref[0] = _rms(acc[...], g_ref[...]).astype(BF16)


def _neighborhood_attention(q3d, k3d, v3d, bias_tbl, g_a):
    b, seq_len, _ = q3d.shape
    n_rows = seq_len // GRID_W
    tq = ROW_GROUP * GRID_W
    chunk = KV_CHUNK_ROWS * GRID_W
    n_chunks = seq_len // chunk

    def kv_spec(c):
        return pl.BlockSpec((1, chunk, NA_WIDTH),
                            lambda bi, g: (bi, jnp.clip(2 * g - 1 + c, 0, n_chunks - 1), 0))

    kern = functools.partial(_na_kernel, n_rows=n_rows)
    return pl.pallas_call(
        kern,
        grid=(b, n_rows // ROW_GROUP),
        in_specs=[pl.BlockSpec((1, tq, NA_WIDTH), lambda bi, g: (bi, g, 0))]
        + [kv_spec(c) for c in range(4)] * 2
        + [_const_spec(bias_tbl.shape), _const_spec((1, NA_WIDTH))],
        out_specs=pl.BlockSpec((1, tq, NA_WIDTH), lambda bi, g: (bi, g, 0)),
        out_shape=jax.ShapeDtypeStruct((b, seq_len, NA_WIDTH), BF16),
        scratch_shapes=[pltpu.VMEM((4 * chunk, NA_WIDTH), BF16), pltpu.VMEM((4 * chunk, NA_WIDTH), BF16),
                        pltpu.VMEM((tq, NA_WIDTH), F32)],
        compiler_params=_params("parallel", "parallel"),
        name="neighborhood_attention",
    )(q3d, k3d, k3d, k3d, k3d, v3d, v3d, v3d, v3d, bias_tbl, g_a)


def _post_kernel(x_ref, a_ref, p_ref, wout_ref, gc_ref, wq_ref, km_ref, vm_ref, wo_ref, gm_ref,
                 wr_hi_ref, wr_lo_ref, br_ref, x2_ref, eid_ref, gate_ref):
    tm = x_ref.shape[1]
    x1 = (x_ref[0]
          + jnp.dot(a_ref[0], wout_ref[0:NA_WIDTH, :], preferred_element_type=F32)
          + jnp.dot(p_ref[0], wout_ref[NA_WIDTH:, :], preferred_element_type=F32))
    h2 = _rms(x1, gc_ref[...]).astype(BF16)
    qm = jnp.dot(h2, wq_ref[...], preferred_element_type=F32).astype(BF16)
    heads = []
    for hd in range(MEM_HEADS):
        cols = slice(hd * MEM_HEAD_DIM, (hd + 1) * MEM_HEAD_DIM)
        s = lax.dot_general(qm[:, cols], km_ref[0, :, cols], (((1,), (1,)), ((), ())),
                            preferred_element_type=F32) * (MEM_HEAD_DIM ** -0.5)
        m = jnp.max(s, axis=-1, keepdims=True)
        p = jnp.exp(s - m)
        l = jnp.sum(p, axis=-1, keepdims=True)
        o = jnp.dot(p.astype(BF16), vm_ref[0, :, cols], preferred_element_type=F32) / l
        heads.append(o.astype(BF16))
    o_all = jnp.concatenate(heads, axis=-1)
    x2 = x1 + jnp.dot(o_all, wo_ref[...], preferred_element_type=F32)
    x2_ref[0] = x2
    h3 = _rms(x2, gm_ref[...])

    h_hi = h3.astype(BF16)
    h_lo = (h3 - h_hi.astype(F32)).astype(BF16)
    logits = (jnp.dot(h_hi, wr_hi_ref[...], preferred_element_type=F32)
              + jnp.dot(h_lo, wr_hi_ref[...], preferred_element_type=F32)
              + jnp.dot(h_hi, wr_lo_ref[...], preferred_element_type=F32)) + br_ref[...]

    lane = lax.broadcasted_iota(jnp.int32, (tm, LANES), 1)
    lane_f = lane.astype(F32)
    big = float(LANES)
    cmask = lane < N_EXPERT_GROUPS
    lc = jnp.where(cmask, logits, NEG)
    cmax = jnp.max(lc, axis=-1, keepdims=True)
    g_idx = jnp.min(jnp.where(lc == cmax, lane_f, big), axis=-1, keepdims=True)
    g_gate = 1.0 / jnp.sum(jnp.exp(lc - cmax), axis=-1, keepdims=True)
    grp_f = ((lane - N_EXPERT_GROUPS) >> 3).astype(F32)
    in_fine = jnp.where(lane >= N_EXPERT_GROUPS, jnp.where(lane < N_EXPERT_GROUPS + N_EXPERTS, grp_f, -1.0), -1.0)
    lf = jnp.where(in_fine == g_idx, logits, NEG)
    v1 = jnp.max(lf, axis=-1, keepdims=True)
    i1 = jnp.min(jnp.where(lf == v1, lane_f, big), axis=-1, keepdims=True)
    lf2 = jnp.where(lane_f == i1, NEG, lf)
    v2 = jnp.max(lf2, axis=-1, keepdims=True)
    i2 = jnp.min(jnp.where(lf2 == v2, lane_f, big), axis=-1, keepdims=True)
    e21 = jnp.exp(v2 - v1)
    den = 1.0 / (1.0 + e21)
    two = lax.broadcasted_iota(jnp.int32, (tm, 2), 1)
    eid_ref[0] = jnp.where(two == 0, i1, i2).astype(jnp.int32) - N_EXPERT_GROUPS
    gate_ref[0] = jnp.where(two == 0, g_gate * den, g_gate * (e21 * den))


def _post_mix(x3d, a_n, p_n, w_out_bf, g_cross, w_q_bf, k_mem, v_mem, w_o_bf, g_moe, wr_hi, wr_lo, b_r, tm=256):
    b, seq_len, _ = x3d.shape
    tok = lambda w: pl.BlockSpec((1, tm, w), lambda bi, i: (bi, i, 0))
    mem = pl.BlockSpec((1, N_MEM, MEM_WIDTH), lambda bi, i: (bi, 0, 0))
    return pl.pallas_call(
        _post_kernel,
        grid=(b, seq_len // tm),
        in_specs=[tok(D_MODEL), tok(NA_WIDTH), tok(POOL_WIDTH), _const_spec(w_out_bf.shape),
                  _const_spec((1, D_MODEL)), _const_spec(w_q_bf.shape), mem, mem, _const_spec(w_o_bf.shape),
                  _const_spec((1, D_MODEL)), _const_spec(wr_hi.shape), _const_spec(wr_lo.shape),
                  _const_spec((1, LANES))],
        out_specs=[tok(D_MODEL), tok(2), tok(2)],
        out_shape=[jax.ShapeDtypeStruct((b, seq_len, D_MODEL), F32),
                   jax.ShapeDtypeStruct((b, seq_len, 2), jnp.int32), jax.ShapeDtypeStruct((b, seq_len, 2), F32)],
        compiler_params=_params("parallel", "parallel"),
        name="post_mix",
    )(x3d, a_n, p_n, w_out_bf, g_cross, w_q_bf, k_mem, v_mem, w_o_bf, g_moe, wr_hi, wr_lo, b_r)


def _rank_kernel(eid_ref, rank_ref, cnt_ref, carry):
    tm = eid_ref.shape[0]

    @pl.when(pl.program_id(0) == 0)
    def _():
        carry[...] = jnp.zeros_like(carry)

    lane = lax.broadcasted_iota(jnp.int32, (tm, LANES), 1)
    e1 = eid_ref[:, 0:1]
    e2 = eid_ref[:, 1:2]
    hit1 = lane == e1
    hit2 = lane == e2
    onehot = jnp.where(hit1, 1.0, 0.0) + jnp.where(hit2, 1.0, 0.0)
    row = lax.broadcasted_iota(jnp.int32, (tm, tm), 0)
    col = lax.broadcasted_iota(jnp.int32, (tm, tm), 1)
    earlier = jnp.where(col < row, 1.0, 0.0).astype(BF16)
    before = jnp.dot(earlier, onehot.astype(BF16), preferred_element_type=F32) + carry[...]
    r1 = jnp.sum(jnp.where(hit1, before, 0.0), axis=-1, keepdims=True)
    r2 = jnp.sum(jnp.where(hit2, before, 0.0), axis=-1, keepdims=True)
    two = lax.broadcasted_iota(jnp.int32, (tm, 2), 1)
    rank_ref[...] = jnp.where(two == 0, r1, r2).astype(jnp.int32)
    total = carry[...] + jnp.sum(onehot, axis=0, keepdims=True)
    carry[...] = total
    cnt_ref[...] = total.astype(jnp.int32)


def _expert_ranks(eid2d, tm=512):
    t = eid2d.shape[0]
    return pl.pallas_call(
        _rank_kernel,
        grid=(t // tm,),
        in_specs=[pl.BlockSpec((tm, 2), lambda i: (i, 0))],
        out_specs=[pl.BlockSpec((tm, 2), lambda i: (i, 0)), pl.BlockSpec((1, LANES), lambda i: (0, 0))],
        out_shape=[jax.ShapeDtypeStruct((t, 2), jnp.int32), jax.ShapeDtypeStruct((1, LANES), jnp.int32)],
        scratch_shapes=[pltpu.VMEM((1, LANES), F32)],
        compiler_params=_params("arbitrary"),
        name="expert_ranks",
    )(eid2d)


def _row_gather(src_ref, idx_ref, idx_base, n_rows, dst_ref, sem):
    for r in range(n_rows):
        pltpu.make_async_copy(src_ref.at[pl.ds(idx_ref[idx_base + r], 1)], dst_ref.at[pl.ds(r, 1)], sem).start()


def _row_gather_wait(src_ref, n_rows, dst_ref, sem):
    for r in range(n_rows):
        pltpu.make_async_copy(src_ref.at[pl.ds(0, 1)], dst_ref.at[pl.ds(r, 1)], sem).wait()


def _expert_kernel(tok_ref, be_ref, x2_ref, g_ref, wgu_ref, wd_ref, ys_ref, xbuf, xn_ref, sems):
    del be_ref
    i = pl.program_id(0)
    n = pl.num_programs(0)
    slot = i % 2

    @pl.when(i == 0)
    def _():
        _row_gather(x2_ref, tok_ref, 0, MOE_BLOCK, xbuf.at[0], sems.at[0])

    _row_gather_wait(x2_ref, MOE_BLOCK, xbuf.at[slot], sems.at[slot])
    xn_ref[...] = _rms(xbuf[slot], g_ref[...]).astype(BF16)
    nxt = jnp.minimum(i + 1, n - 1)
    _row_gather(x2_ref, tok_ref, nxt * MOE_BLOCK, MOE_BLOCK, xbuf.at[1 - slot], sems.at[1 - slot])
    gu = jnp.dot(xn_ref[...], wgu_ref[0], preferred_element_type=F32)
    gate = gu[:, :D_EXPERT]
    hid = gate * jax.nn.sigmoid(gate) * gu[:, D_EXPERT:]
    ys_ref[...] = jnp.dot(hid.astype(BF16), wd_ref[0], preferred_element_type=F32)

    @pl.when(i == n - 1)
    def _():
        _row_gather_wait(x2_ref, MOE_BLOCK, xbuf.at[1 - slot], sems.at[1 - slot])


def _experts(slot_tok, block_expert, x2_2d, g_moe, w_gu_bf, w_d_bf):
    n_slots = slot_tok.shape[0]
    d = x2_2d.shape[1]
    return pl.pallas_call(
        _expert_kernel,
        grid_spec=pltpu.PrefetchScalarGridSpec(
            num_scalar_prefetch=2,
            grid=(n_slots // MOE_BLOCK,),
            in_specs=[pl.BlockSpec(memory_space=pl.ANY),
                      pl.BlockSpec((1, d), lambda i, tok, be: (0, 0)),
                      pl.BlockSpec((1, d, 2 * D_EXPERT), lambda i, tok, be: (be[i], 0, 0)),
                      pl.BlockSpec((1, D_EXPERT, d), lambda i, tok, be: (be[i], 0, 0))],
            out_specs=pl.BlockSpec((MOE_BLOCK, d), lambda i, tok, be: (i, 0)),
            scratch_shapes=[pltpu.VMEM((2, MOE_BLOCK, d), F32), pltpu.VMEM((MOE_BLOCK, d), BF16),
                            pltpu.SemaphoreType.DMA((2,))],
        ),
        out_shape=jax.ShapeDtypeStruct((n_slots, d), F32),
        compiler_params=_params("arbitrary"),
        name="moe_experts",
    )(slot_tok, block_expert, x2_2d, g_moe, w_gu_bf, w_d_bf)


def _combine_kernel(dest_ref, x2_ref, gate_ref, g_ref, ys_ref, o_ref, ybuf, sems, *, tm):
    i = pl.program_id(0)
    n = pl.num_programs(0)
    slot = i % 2
    unroll = 8

    n_tok = n * tm

    def start_tile(tile, buf_slot):
        def issue(c, carry):
            for k in range(2):
                _row_gather(ys_ref, dest_ref, k * n_tok + tile * tm + c * unroll, unroll,
                            ybuf.at[buf_slot, k, pl.ds(c * unroll, unroll)], sems.at[buf_slot])
            return carry

        lax.fori_loop(0, tm // unroll, issue, 0)

    @pl.when(i == 0)
    def _():
        start_tile(0, 0)

    @pl.when(i + 1 < n)
    def _():
        start_tile(i + 1, 1 - slot)

    def drain(c, carry):
        for k in range(2):
            _row_gather_wait(ys_ref, unroll, ybuf.at[slot, k, pl.ds(c * unroll, unroll)], sems.at[slot])
        return carry

    lax.fori_loop(0, tm // unroll, drain, 0)
    gates = gate_ref[...]
    y = gates[:, 0:1] * ybuf[slot, 0] + gates[:, 1:2] * ybuf[slot, 1]
    o_ref[...] = _rms(x2_ref[...] + y, g_ref[...])


def _combine(dest_flat, x2_2d, gate2d, g_final, ys, tm=256):
    t, d = x2_2d.shape
    kern = functools.partial(_combine_kernel, tm=tm)
    return pl.pallas_call(
        kern,
        grid_spec=pltpu.PrefetchScalarGridSpec(
            num_scalar_prefetch=1,
            grid=(t // tm,),
            in_specs=[pl.BlockSpec((tm, d), lambda i, dst: (i, 0)),
                      pl.BlockSpec((tm, 2), lambda i, dst: (i, 0)),
                      pl.BlockSpec((1, d), lambda i, dst: (0, 0)),
                      pl.BlockSpec(memory_space=pl.ANY)],
            out_specs=pl.BlockSpec((tm, d), lambda i, dst: (i, 0)),
            scratch_shapes=[pltpu.VMEM((2, 2, tm, d), F32), pltpu.SemaphoreType.DMA((2,))],
        ),
        out_shape=jax.ShapeDtypeStruct((t, d), F32),
        compiler_params=_params("arbitrary"),
        name="moe_combine",
    )(dest_flat, x2_2d, gate2d, g_final, ys)


def _encode(x, mem, p):
    b, seq_len, d = x.shape
    t = b * seq_len
    q, k, v, u = _in_proj(x.reshape(t, d), p["g_mix"], p["w_in"])
    shape3 = (b, seq_len, NA_WIDTH)
    a_n = _neighborhood_attention(q.reshape(shape3), k.reshape(shape3), v.reshape(shape3), p["na_bias"], p["g_branch_a"])
    p_n = _pool_mixer(u.reshape(shape3), p["w_pool"], p["pool_scale"], p["g_branch_b"])
    kv = _norm_matmul(mem.reshape(b * N_MEM, d), p["g_mem"], p["w_kv"], BF16).reshape(b, N_MEM, 2 * MEM_WIDTH)
    x2, eid, gate = _post_mix(x, a_n, p_n, p["w_out"], p["g_cross"], p["w_q"], kv[:, :, :MEM_WIDTH],
                              kv[:, :, MEM_WIDTH:], p["w_o"], p["g_moe"], p["wr_hi"], p["wr_lo"], p["b_r"])
    x2 = x2.reshape(t, d)
    eid2d = eid.reshape(t, 2)
    rank, counts = _expert_ranks(eid2d)
    counts = counts[0, :N_EXPERTS]
    padded = ((counts + MOE_BLOCK - 1) // MOE_BLOCK) * MOE_BLOCK
    padded_ends = jnp.cumsum(padded)
    padded_starts = padded_ends - padded
    n_blocks = -(-(2 * t + N_EXPERTS * (MOE_BLOCK - 1)) // MOE_BLOCK)
    n_slots = n_blocks * MOE_BLOCK
    block_pos = jnp.arange(n_blocks, dtype=jnp.int32) * MOE_BLOCK
    block_expert = jnp.minimum(jnp.sum(padded_ends[None, :] <= block_pos[:, None], axis=1), N_EXPERTS - 1).astype(jnp.int32)
    is_e = eid2d[:, :, None] == jnp.arange(N_EXPERTS, dtype=jnp.int32)
    dest = (jnp.sum(jnp.where(is_e, padded_starts, 0), axis=-1) + rank).astype(jnp.int32)
    dest_kmajor = dest.T.reshape(-1)
    token_of = jnp.broadcast_to(jnp.arange(t, dtype=jnp.int32)[None, :], (2, t)).reshape(-1)
    slot_tok = jnp.zeros((n_slots,), jnp.int32).at[dest_kmajor].set(token_of, unique_indices=True)
    ys = _experts(slot_tok, block_expert, x2, p["g_moe"], p["w_gu"], p["w_d"])
    y = _combine(dest_kmajor, x2, gate.reshape(t, 2), p["g_final"], ys)
    return y.reshape(b, seq_len, d)


def kernel(x_prompt, x_sample, mem_prompt, mem_sample, g_mix, w_in, rpb, w_pool, pool_scale, g_branch_a, g_branch_b, w_out, g_cross, g_mem, w_q_mem, w_kv_mem, w_o_mem, g_moe, w_coarse, b_coarse, w_fine, b_fine, w_gate_e, w_up_e, w_down_e, g_final):
    depth = w_in.shape[0]
    xp, xs = x_prompt, x_sample
    row = lambda a: a.reshape(1, -1).astype(F32)
    for l in range(depth):
        w_r = jnp.concatenate([w_coarse[l], w_fine[l]], axis=1).astype(F32)
        w_r = jnp.pad(w_r, ((0, 0), (0, LANES - w_r.shape[1])))
        wr_hi = w_r.astype(BF16)
        b_r = jnp.concatenate([b_coarse[l], b_fine[l]]).astype(F32)
        p = {
            "g_mix": row(g_mix[l]), "w_in": w_in[l].astype(BF16),
            "na_bias": _na_bias_table(rpb[l]), "g_branch_a": row(g_branch_a[l]),
            "w_pool": w_pool[l].astype(BF16), "pool_scale": row(pool_scale[l]), "g_branch_b": row(g_branch_b[l]),
            "w_out": w_out[l].astype(BF16), "g_cross": row(g_cross[l]), "g_mem": row(g_mem[l]),
            "w_q": w_q_mem[l].astype(BF16), "w_kv": w_kv_mem[l].astype(BF16), "w_o": w_o_mem[l].astype(BF16),
            "g_moe": row(g_moe[l]),
            "wr_hi": wr_hi, "wr_lo": (w_r - wr_hi.astype(F32)).astype(BF16),
            "b_r": jnp.pad(b_r, (0, LANES - b_r.shape[0])).reshape(1, LANES),
            "w_gu": jnp.concatenate([w_gate_e[l], w_up_e[l]], axis=-1).astype(BF16),
            "w_d": w_down_e[l].astype(BF16),
            "g_final": row(g_final),
        }
        assert depth == 1, "final-norm fusion assumes a single layer"
        xp = _encode(xp, mem_prompt, p)
        xs = _encode(xs, mem_sample, p)
    return (xp, xs)
```

```python
import functools

import jax
import jax.numpy as jnp
from jax import lax
from jax.experimental import pallas as pl
from jax.experimental.pallas import tpu as pltpu

F32 = jnp.float32
BF16 = jnp.bfloat16

D_MODEL = 2048
GRID_W = 64
NA_HEADS = 16
NA_WIDTH = 1024
NA_HEAD_DIM = 64
WIN_H = 8
WIN_W = 16
POOL_WINDOWS = (2, 4, 8, 16)
POOL_WIDTH = 1024
POOL_GROUP_DIM = 256
POOL_HALO = 16
N_MEM = 256
MEM_HEADS = 4
MEM_HEAD_DIM = 128
MEM_WIDTH = 512
N_EXPERT_GROUPS = 4
EXPERTS_PER_GROUP = 8
N_EXPERTS = 32
D_EXPERT = 512
MOE_BLOCK = 256
EPS = 1e-6
NEG = -1e30
LANES = 128
ROW_GROUP = 8
KV_CHUNK_ROWS = 4
SOFTMAX_ROWS = 16
VMEM_LIMIT = 56 * 1024 * 1024


def _rms(x, g):
    return x * lax.rsqrt(jnp.mean(x * x, axis=-1, keepdims=True) + EPS) * g


def _const_spec(shape):
    nd = len(shape)
    return pl.BlockSpec(shape, lambda *_: (0,) * nd, pipeline_mode=pl.Buffered(1))


def _params(*sem):
    return pltpu.CompilerParams(dimension_semantics=sem, vmem_limit_bytes=VMEM_LIMIT)


def _in_proj_kernel(x_ref, g_ref, w_ref, q_ref, k_ref, v_ref, u_ref):
    h = _rms(x_ref[...], g_ref[...]).astype(BF16)
    q_ref[...] = (jnp.dot(h, w_ref[:, 0:NA_WIDTH], preferred_element_type=F32)
                  * (NA_HEAD_DIM ** -0.5)).astype(BF16)
    k_ref[...] = jnp.dot(h, w_ref[:, NA_WIDTH:2 * NA_WIDTH], preferred_element_type=F32).astype(BF16)
    v_ref[...] = jnp.dot(h, w_ref[:, 2 * NA_WIDTH:3 * NA_WIDTH], preferred_element_type=F32).astype(BF16)
    u_ref[...] = jnp.dot(h, w_ref[:, 3 * NA_WIDTH:], preferred_element_type=F32)


def _in_proj(x2d, g_mix, w_in_bf, tm=512):
    t = x2d.shape[0]
    row = lambda i: (i, 0)
    return pl.pallas_call(
        _in_proj_kernel,
        grid=(t // tm,),
        in_specs=[pl.BlockSpec((tm, D_MODEL), row), _const_spec((1, D_MODEL)),
                  _const_spec(w_in_bf.shape)],
        out_specs=[pl.BlockSpec((tm, NA_WIDTH), row)] * 4,
        out_shape=[jax.ShapeDtypeStruct((t, NA_WIDTH), BF16)] * 3
        + [jax.ShapeDtypeStruct((t, POOL_WIDTH), F32)],
        compiler_params=_params("parallel"),
        name="in_proj",
    )(x2d, g_mix, w_in_bf)


def _norm_matmul_kernel(x_ref, g_ref, w_ref, o_ref):
    h = _rms(x_ref[...], g_ref[...]).astype(BF16)
    o_ref[...] = jnp.dot(h, w_ref[...], preferred_element_type=F32).astype(o_ref.dtype)


def _norm_matmul(x2d, g, w_bf, out_dtype, tm=256):
    t, k = x2d.shape
    n = w_bf.shape[1]
    return pl.pallas_call(
        _norm_matmul_kernel,
        grid=(t // tm,),
        in_specs=[pl.BlockSpec((tm, k), lambda i: (i, 0)), _const_spec((1, k)), _const_spec((k, n))],
        out_specs=pl.BlockSpec((tm, n), lambda i: (i, 0)),
        out_shape=jax.ShapeDtypeStruct((t, n), out_dtype),
        compiler_params=_params("parallel"),
        name="norm_matmul",
    )(x2d, g, w_bf)


def _pool_kernel(prev_ref, cur_ref, next_ref, w_ref, scale_ref, g_ref, o_ref, ext_ref, mix_ref, *, seq_len, tm):
    i = pl.program_id(1)
    n_i = pl.num_programs(1)
    ext_ref[0:POOL_HALO, :] = jnp.where(i > 0, prev_ref[0], 0.0)
    ext_ref[POOL_HALO:POOL_HALO + tm, :] = cur_ref[0]
    ext_ref[POOL_HALO + tm:, :] = jnp.where(i < n_i - 1, next_ref[0], 0.0)
    n_ext = tm + 2 * POOL_HALO
    pos = i * tm + lax.broadcasted_iota(jnp.int32, (tm, 1), 0)
    for gi, w in enumerate(POOL_WINDOWS):
        cols = slice(gi * POOL_GROUP_DIM, (gi + 1) * POOL_GROUP_DIM)
        e = ext_ref[:, cols]
        s = e + pltpu.roll(e, 1, 0)
        half = 1
        while 2 * half < w:
            s = pltpu.roll(s, half, 0) + pltpu.roll(s, n_ext - half, 0)
            half *= 2
        lo = jnp.maximum(pos - w // 2, 0)
        hi = jnp.minimum(pos - w // 2 + w, seq_len)
        cnt = (hi - lo).astype(F32)
        pooled = s[POOL_HALO:POOL_HALO + tm] / cnt - e[POOL_HALO:POOL_HALO + tm]
        mix_ref[:, cols] = jnp.dot(pooled.astype(BF16), w_ref[gi], preferred_element_type=F32)
    mixed = mix_ref[...] * scale_ref[...]
    o_ref[0] = _rms(mixed, g_ref[...]).astype(BF16)


def _pool_mixer(u3d, w_pool_bf, pool_scale, g_b, tm=512):
    b, seq_len, _ = u3d.shape
    tm = min(tm, seq_len)
    hb = tm // POOL_HALO
    n_hb = seq_len // POOL_HALO
    kern = functools.partial(_pool_kernel, seq_len=seq_len, tm=tm)
    return pl.pallas_call(
        kern,
        grid=(b, seq_len // tm),
        in_specs=[
            pl.BlockSpec((1, POOL_HALO, POOL_WIDTH), lambda bi, i: (bi, jnp.maximum(i * hb - 1, 0), 0)),
            pl.BlockSpec((1, tm, POOL_WIDTH), lambda bi, i: (bi, i, 0)),
            pl.BlockSpec((1, POOL_HALO, POOL_WIDTH), lambda bi, i: (bi, jnp.minimum((i + 1) * hb, n_hb - 1), 0)),
            _const_spec(w_pool_bf.shape), _const_spec((1, POOL_WIDTH)), _const_spec((1, POOL_WIDTH)),
        ],
        out_specs=pl.BlockSpec((1, tm, POOL_WIDTH), lambda bi, i: (bi, i, 0)),
        out_shape=jax.ShapeDtypeStruct((b, seq_len, POOL_WIDTH), BF16),
        scratch_shapes=[pltpu.VMEM((tm + 2 * POOL_HALO, POOL_WIDTH), F32), pltpu.VMEM((tm, POOL_WIDTH), F32)],
        compiler_params=_params("parallel", "parallel"),
        name="pool_mixer",
    )(u3d, u3d, u3d, w_pool_bf, pool_scale, g_b)


def _na_bias_table(rpb):
    c = jnp.arange(GRID_W)[:, None]
    kc = jnp.arange(GRID_W)[None, :]
    start = jnp.clip(c - WIN_W // 2, 0, GRID_W - WIN_W)
    valid = (kc >= start) & (kc < start + WIN_W)
    dc = kc - c + (WIN_W - 1)
    onehot = (dc[:, :, None] == jnp.arange(2 * WIN_W - 1)[None, None, :]).astype(F32)
    band = jnp.einsum("hrd,cqd->hrcq", rpb.astype(F32), onehot, precision=lax.Precision.HIGHEST)
    band = jnp.where(valid[None, None], band, NEG)
    pats = []
    for pat in range(WIN_H):
        win = band[:, WIN_H - 1 - pat:2 * WIN_H - 1 - pat]
        pats.append(jnp.transpose(win, (0, 2, 1, 3)).reshape(NA_HEADS // 2, 2 * GRID_W, WIN_H * GRID_W))
    return jnp.stack(pats)


def _na_kernel(q_ref, k0, k1, k2, k3, v0, v1, v2, v3, bias_ref, g_ref, o_ref, kbuf, vbuf, acc, s_scr, p_scr, *, n_rows):
    g = pl.program_id(1)
    chunk = KV_CHUNK_ROWS * GRID_W
    for c, (kr, vr) in enumerate(((k0, v0), (k1, v1), (k2, v2), (k3, v3))):
        kbuf[c * chunk:(c + 1) * chunk, :] = kr[0]
        vbuf[c * chunk:(c + 1) * chunk, :] = vr[0]
    first_head = lax.broadcasted_iota(jnp.int32, (GRID_W, LANES), 1) < NA_HEAD_DIM
    n_keys = WIN_H * GRID_W

    def row_body(j, carry):
        r = g * ROW_GROUP + j
        r0 = jnp.clip(r - WIN_H // 2, 0, n_rows - WIN_H)
        pat = r - r0
        koff = pl.multiple_of((r0 - (g * ROW_GROUP - KV_CHUNK_ROWS)) * GRID_W, GRID_W)
        qoff = pl.multiple_of(j * GRID_W, GRID_W)
        n_pairs = NA_HEADS // 2
        for hp in range(n_pairs):
            cols = slice(hp * LANES, (hp + 1) * LANES)
            q2 = q_ref[0, pl.ds(qoff, GRID_W), cols]
            zero = jnp.zeros_like(q2)
            qq = jnp.concatenate([jnp.where(first_head, q2, zero), jnp.where(first_head, zero, q2)], axis=0)
            kk = kbuf[pl.ds(koff, n_keys), cols]
            s_scr[hp] = lax.dot_general(qq, kk, (((1,), (1,)), ((), ())), preferred_element_type=F32)
        for hp in range(n_pairs):
            for rc in range(2 * GRID_W // SOFTMAX_ROWS):
                rows = slice(rc * SOFTMAX_ROWS, (rc + 1) * SOFTMAX_ROWS)
                s = s_scr[hp, rows, :] + bias_ref[pat, hp, rows, :]
                p = jnp.exp(s - jnp.max(s, axis=-1, keepdims=True))
                p = p * (1.0 / jnp.sum(p, axis=-1, keepdims=True))
                p_scr[hp, rows, :] = p.astype(BF16)
        for hp in range(n_pairs):
            cols = slice(hp * LANES, (hp + 1) * LANES)
            vv = vbuf[pl.ds(koff, n_keys), cols]
            o = jnp.dot(p_scr[hp], vv, preferred_element_type=F32)
            acc[pl.ds(qoff, GRID_W), cols] = jnp.where(first_head, o[:GRID_W], o[GRID_W:])
        return carry

    lax.fori_loop(0, ROW_GROUP, row_body, 0)
    o_ref[0] = _rms(acc[...], g_ref[...]).astype(BF16)


def _neighborhood_attention(q3d, k3d, v3d, bias_tbl, g_a):
    b, seq_len, _ = q3d.shape
    n_rows = seq_len // GRID_W
    tq = ROW_GROUP * GRID_W
    chunk = KV_CHUNK_ROWS * GRID_W
    n_chunks = seq_len // chunk

    def kv_spec(c):
        return pl.BlockSpec((1, chunk, NA_WIDTH),
                            lambda bi, g: (bi, jnp.clip(2 * g - 1 + c, 0, n_chunks - 1), 0))

    kern = functools.partial(_na_kernel, n_rows=n_rows)
    return pl.pallas_call(
        kern,
        grid=(b, n_rows // ROW_GROUP),
        in_specs=[pl.BlockSpec((1, tq, NA_WIDTH), lambda bi, g: (bi, g, 0))]
        + [kv_spec(c) for c in range(4)] * 2
        + [_const_spec(bias_tbl.shape), _const_spec((1, NA_WIDTH))],
        out_specs=pl.BlockSpec((1, tq, NA_WIDTH), lambda bi, g: (bi, g, 0)),
        out_shape=jax.ShapeDtypeStruct((b, seq_len, NA_WIDTH), BF16),
        scratch_shapes=[pltpu.VMEM((4 * chunk, NA_WIDTH), BF16), pltpu.VMEM((4 * chunk, NA_WIDTH), BF16),
                        pltpu.VMEM((tq, NA_WIDTH), F32),
                        pltpu.VMEM((NA_HEADS // 2, 2 * GRID_W, WIN_H * GRID_W), F32),
                        pltpu.VMEM((NA_HEADS // 2, 2 * GRID_W, WIN_H * GRID_W), BF16)],
        compiler_params=_params("parallel", "parallel"),
        name="neighborhood_attention",
    )(q3d, k3d, k3d, k3d, k3d, v3d, v3d, v3d, v3d, bias_tbl, g_a)


def _post_kernel(x_ref, a_ref, p_ref, wout_ref, gc_ref, wq_ref, km_ref, vm_ref, wo_ref, gm_ref,
                 wr_hi_ref, wr_lo_ref, br_ref, x2_ref, eid_ref, gate_ref):
    tm = x_ref.shape[1]
    x1 = (x_ref[0]
          + jnp.dot(a_ref[0], wout_ref[0:NA_WIDTH, :], preferred_element_type=F32)
          + jnp.dot(p_ref[0], wout_ref[NA_WIDTH:, :], preferred_element_type=F32))
    h2 = _rms(x1, gc_ref[...]).astype(BF16)
    qm = jnp.dot(h2, wq_ref[...], preferred_element_type=F32).astype(BF16)
    heads = []
    for hd in range(MEM_HEADS):
        cols = slice(hd * MEM_HEAD_DIM, (hd + 1) * MEM_HEAD_DIM)
        s = lax.dot_general(qm[:, cols], km_ref[0, :, cols], (((1,), (1,)), ((), ())),
                            preferred_element_type=F32) * (MEM_HEAD_DIM ** -0.5)
        m = jnp.max(s, axis=-1, keepdims=True)
        p = jnp.exp(s - m)
        l = jnp.sum(p, axis=-1, keepdims=True)
        o = jnp.dot(p.astype(BF16), vm_ref[0, :, cols], preferred_element_type=F32) / l
        heads.append(o.astype(BF16))
    o_all = jnp.concatenate(heads, axis=-1)
    x2 = x1 + jnp.dot(o_all, wo_ref[...], preferred_element_type=F32)
    x2_ref[0] = x2
    h3 = _rms(x2, gm_ref[...])

    h_hi = h3.astype(BF16)
    h_lo = (h3 - h_hi.astype(F32)).astype(BF16)
    nt = (((1,), (1,)), ((), ()))
    logits = (lax.dot_general(h_hi, wr_hi_ref[...], nt, preferred_element_type=F32)
              + lax.dot_general(h_lo, wr_hi_ref[...], nt, preferred_element_type=F32)
              + lax.dot_general(h_hi, wr_lo_ref[...], nt, preferred_element_type=F32)) + br_ref[...]

    lane = lax.broadcasted_iota(jnp.int32, (tm, LANES), 1)
    lane_f = lane.astype(F32)
    big = float(LANES)
    cmask = lane < N_EXPERT_GROUPS
    lc = jnp.where(cmask, logits, NEG)
    cmax = jnp.max(lc, axis=-1, keepdims=True)
    g_idx = jnp.min(jnp.where(lc == cmax, lane_f, big), axis=-1, keepdims=True)
    g_gate = 1.0 / jnp.sum(jnp.exp(lc - cmax), axis=-1, keepdims=True)
    grp_f = ((lane - N_EXPERT_GROUPS) >> 3).astype(F32)
    in_fine = jnp.where(lane >= N_EXPERT_GROUPS, jnp.where(lane < N_EXPERT_GROUPS + N_EXPERTS, grp_f, -1.0), -1.0)
    lf = jnp.where(in_fine == g_idx, logits, NEG)
    v1 = jnp.max(lf, axis=-1, keepdims=True)
    i1 = jnp.min(jnp.where(lf == v1, lane_f, big), axis=-1, keepdims=True)
    lf2 = jnp.where(lane_f == i1, NEG, lf)
    v2 = jnp.max(lf2, axis=-1, keepdims=True)
    i2 = jnp.min(jnp.where(lf2 == v2, lane_f, big), axis=-1, keepdims=True)
    e21 = jnp.exp(v2 - v1)
    den = 1.0 / (1.0 + e21)
    two = lax.broadcasted_iota(jnp.int32, (tm, 2), 1)
    eid_ref[0] = jnp.where(two == 0, i1, i2).astype(jnp.int32) - N_EXPERT_GROUPS
    gate_ref[0] = jnp.where(two == 0, g_gate * den, g_gate * (e21 * den))


def _post_mix(x3d, a_n, p_n, w_out_bf, g_cross, w_q_bf, k_mem, v_mem, w_o_bf, g_moe, wr_hi, wr_lo, b_r, tm=256):
    b, seq_len, _ = x3d.shape
    tok = lambda w: pl.BlockSpec((1, tm, w), lambda bi, i: (bi, i, 0))
    mem = pl.BlockSpec((1, N_MEM, MEM_WIDTH), lambda bi, i: (bi, 0, 0))
    return pl.pallas_call(
        _post_kernel,
        grid=(b, seq_len // tm),
        in_specs=[tok(D_MODEL), tok(NA_WIDTH), tok(POOL_WIDTH), _const_spec(w_out_bf.shape),
                  _const_spec((1, D_MODEL)), _const_spec(w_q_bf.shape), mem, mem, _const_spec(w_o_bf.shape),
                  _const_spec((1, D_MODEL)), _const_spec(wr_hi.shape), _const_spec(wr_lo.shape),
                  _const_spec((1, LANES))],
        out_specs=[tok(D_MODEL), tok(2), tok(2)],
        out_shape=[jax.ShapeDtypeStruct((b, seq_len, D_MODEL), F32),
                   jax.ShapeDtypeStruct((b, seq_len, 2), jnp.int32), jax.ShapeDtypeStruct((b, seq_len, 2), F32)],
        compiler_params=_params("parallel", "parallel"),
        name="post_mix",
    )(x3d, a_n, p_n, w_out_bf, g_cross, w_q_bf, k_mem, v_mem, w_o_bf, g_moe, wr_hi, wr_lo, b_r)


def _rank_kernel(eid_ref, rank_ref, cnt_ref, carry):
    tm = eid_ref.shape[0]

    @pl.when(pl.program_id(0) == 0)
    def _():
        carry[...] = jnp.zeros_like(carry)

    lane = lax.broadcasted_iota(jnp.int32, (tm, LANES), 1)
    e1 = eid_ref[:, 0:1]
    e2 = eid_ref[:, 1:2]
    hit1 = lane == e1
    hit2 = lane == e2
    onehot = jnp.where(hit1, 1.0, 0.0) + jnp.where(hit2, 1.0, 0.0)
    row = lax.broadcasted_iota(jnp.int32, (tm, tm), 0)
    col = lax.broadcasted_iota(jnp.int32, (tm, tm), 1)
    earlier = jnp.where(col < row, 1.0, 0.0).astype(BF16)
    before = jnp.dot(earlier, onehot.astype(BF16), preferred_element_type=F32) + carry[...]
    r1 = jnp.sum(jnp.where(hit1, before, 0.0), axis=-1, keepdims=True)
    r2 = jnp.sum(jnp.where(hit2, before, 0.0), axis=-1, keepdims=True)
    two = lax.broadcasted_iota(jnp.int32, (tm, 2), 1)
    rank_ref[...] = jnp.where(two == 0, r1, r2).astype(jnp.int32)
    total = carry[...] + jnp.sum(onehot, axis=0, keepdims=True)
    carry[...] = total
    cnt_ref[...] = total.astype(jnp.int32)


def _expert_ranks(eid2d, tm=512):
    t = eid2d.shape[0]
    return pl.pallas_call(
        _rank_kernel,
        grid=(t // tm,),
        in_specs=[pl.BlockSpec((tm, 2), lambda i: (i, 0))],
        out_specs=[pl.BlockSpec((tm, 2), lambda i: (i, 0)), pl.BlockSpec((1, LANES), lambda i: (0, 0))],
        out_shape=[jax.ShapeDtypeStruct((t, 2), jnp.int32), jax.ShapeDtypeStruct((1, LANES), jnp.int32)],
        scratch_shapes=[pltpu.VMEM((1, LANES), F32)],
        compiler_params=_params("arbitrary"),
        name="expert_ranks",
    )(eid2d)


def _row_gather(src_ref, idx_ref, idx_base, n_rows, dst_ref, sem):
    for r in range(n_rows):
        pltpu.make_async_copy(src_ref.at[pl.ds(idx_ref[idx_base + r], 1)], dst_ref.at[pl.ds(r, 1)], sem).start()


def _row_gather_wait(src_ref, n_rows, dst_ref, sem):
    for r in range(n_rows):
        pltpu.make_async_copy(src_ref.at[pl.ds(0, 1)], dst_ref.at[pl.ds(r, 1)], sem).wait()


def _expert_kernel(tok_ref, be_ref, x2_ref, g_ref, wgu_ref, wd_ref, ys_ref, xbuf, xn_ref, sems):
    del be_ref
    i = pl.program_id(0)
    n = pl.num_programs(0)
    slot = i % 2

    unroll = 16

    def start_block(block, buf_slot):
        def issue(c, carry):
            _row_gather(x2_ref, tok_ref, block * MOE_BLOCK + c * unroll, unroll,
                        xbuf.at[buf_slot, pl.ds(c * unroll, unroll)], sems.at[buf_slot])
            return carry

        lax.fori_loop(0, MOE_BLOCK // unroll, issue, 0)

    @pl.when(i == 0)
    def _():
        start_block(0, 0)

    start_block(jnp.minimum(i + 1, n - 1), 1 - slot)
    _row_gather_wait(x2_ref, MOE_BLOCK, xbuf.at[slot], sems.at[slot])
    xn_ref[...] = _rms(xbuf[slot], g_ref[...]).astype(BF16)
    gu = jnp.dot(xn_ref[...], wgu_ref[0], preferred_element_type=F32)
    gate = gu[:, :D_EXPERT]
    hid = gate * jax.nn.sigmoid(gate) * gu[:, D_EXPERT:]
    ys_ref[...] = jnp.dot(hid.astype(BF16), wd_ref[0], preferred_element_type=F32)

    @pl.when(i == n - 1)
    def _():
        _row_gather_wait(x2_ref, MOE_BLOCK, xbuf.at[1 - slot], sems.at[1 - slot])


def _experts(slot_tok, block_expert, x2_2d, g_moe, w_gu_bf, w_d_bf):
    n_slots = slot_tok.shape[0]
    d = x2_2d.shape[1]
    return pl.pallas_call(
        _expert_kernel,
        grid_spec=pltpu.PrefetchScalarGridSpec(
            num_scalar_prefetch=2,
            grid=(n_slots // MOE_BLOCK,),
            in_specs=[pl.BlockSpec(memory_space=pl.ANY),
                      pl.BlockSpec((1, d), lambda i, tok, be: (0, 0)),
                      pl.BlockSpec((1, d, 2 * D_EXPERT), lambda i, tok, be: (be[i], 0, 0)),
                      pl.BlockSpec((1, D_EXPERT, d), lambda i, tok, be: (be[i], 0, 0))],
            out_specs=pl.BlockSpec((MOE_BLOCK, d), lambda i, tok, be: (i, 0)),
            scratch_shapes=[pltpu.VMEM((2, MOE_BLOCK, d), F32), pltpu.VMEM((MOE_BLOCK, d), BF16),
                            pltpu.SemaphoreType.DMA((2,))],
        ),
        out_shape=jax.ShapeDtypeStruct((n_slots, d), F32),
        compiler_params=_params("arbitrary"),
        name="moe_experts",
    )(slot_tok, block_expert, x2_2d, g_moe, w_gu_bf, w_d_bf)


def _combine_kernel(dest_ref, x2_ref, gate_ref, g_ref, ys_ref, o_ref, ybuf, sems, *, tm):
    i = pl.program_id(0)
    n = pl.num_programs(0)
    slot = i % 2
    unroll = 8

    n_tok = n * tm

    def start_tile(tile, buf_slot):
        def issue(c, carry):
            for k in range(2):
                _row_gather(ys_ref, dest_ref, k * n_tok + tile * tm + c * unroll, unroll,
                            ybuf.at[buf_slot, k, pl.ds(c * unroll, unroll)], sems.at[buf_slot])
            return carry

        lax.fori_loop(0, tm // unroll, issue, 0)

    @pl.when(i == 0)
    def _():
        start_tile(0, 0)

    @pl.when(i + 1 < n)
    def _():
        start_tile(i + 1, 1 - slot)

    def drain(c, carry):
        for k in range(2):
            _row_gather_wait(ys_ref, unroll, ybuf.at[slot, k, pl.ds(c * unroll, unroll)], sems.at[slot])
        return carry

    lax.fori_loop(0, tm // unroll, drain, 0)
    gates = gate_ref[...]
    y = gates[:, 0:1] * ybuf[slot, 0] + gates[:, 1:2] * ybuf[slot, 1]
    o_ref[...] = _rms(x2_ref[...] + y, g_ref[...])


def _combine(dest_flat, x2_2d, gate2d, g_final, ys, tm=256):
    t, d = x2_2d.shape
    kern = functools.partial(_combine_kernel, tm=tm)
    return pl.pallas_call(
        kern,
        grid_spec=pltpu.PrefetchScalarGridSpec(
            num_scalar_prefetch=1,
            grid=(t // tm,),
            in_specs=[pl.BlockSpec((tm, d), lambda i, dst: (i, 0)),
                      pl.BlockSpec((tm, 2), lambda i, dst: (i, 0)),
                      pl.BlockSpec((1, d), lambda i, dst: (0, 0)),
                      pl.BlockSpec(memory_space=pl.ANY)],
            out_specs=pl.BlockSpec((tm, d), lambda i, dst: (i, 0)),
            scratch_shapes=[pltpu.VMEM((2, 2, tm, d), F32), pltpu.SemaphoreType.DMA((2,))],
        ),
        out_shape=jax.ShapeDtypeStruct((t, d), F32),
        compiler_params=_params("arbitrary"),
        name="moe_combine",
    )(dest_flat, x2_2d, gate2d, g_final, ys)


def _encode(x, mem, p):
    b, seq_len, d = x.shape
    t = b * seq_len
    q, k, v, u = _in_proj(x.reshape(t, d), p["g_mix"], p["w_in"])
    shape3 = (b, seq_len, NA_WIDTH)
    a_n = _neighborhood_attention(q.reshape(shape3), k.reshape(shape3), v.reshape(shape3), p["na_bias"], p["g_branch_a"])
    p_n = _pool_mixer(u.reshape(shape3), p["w_pool"], p["pool_scale"], p["g_branch_b"])
    kv = _norm_matmul(mem.reshape(b * N_MEM, d), p["g_mem"], p["w_kv"], BF16).reshape(b, N_MEM, 2 * MEM_WIDTH)
    x2, eid, gate = _post_mix(x, a_n, p_n, p["w_out"], p["g_cross"], p["w_q"], kv[:, :, :MEM_WIDTH],
                              kv[:, :, MEM_WIDTH:], p["w_o"], p["g_moe"], p["wr_hi"], p["wr_lo"], p["b_r"])
    x2 = x2.reshape(t, d)
    eid2d = eid.reshape(t, 2)
    rank, counts = _expert_ranks(eid2d)
    counts = counts[0, :N_EXPERTS]
    padded = ((counts + MOE_BLOCK - 1) // MOE_BLOCK) * MOE_BLOCK
    padded_ends = jnp.cumsum(padded)
    padded_starts = padded_ends - padded
    n_blocks = -(-(2 * t + N_EXPERTS * (MOE_BLOCK - 1)) // MOE_BLOCK)
    n_slots = n_blocks * MOE_BLOCK
    block_pos = jnp.arange(n_blocks, dtype=jnp.int32) * MOE_BLOCK
    block_expert = jnp.minimum(jnp.sum(padded_ends[None, :] <= block_pos[:, None], axis=1), N_EXPERTS - 1).astype(jnp.int32)
    is_e = eid2d[:, :, None] == jnp.arange(N_EXPERTS, dtype=jnp.int32)
    dest = (jnp.sum(jnp.where(is_e, padded_starts, 0), axis=-1) + rank).astype(jnp.int32)
    dest_kmajor = dest.T.reshape(-1)
    token_of = jnp.broadcast_to(jnp.arange(t, dtype=jnp.int32)[None, :], (2, t)).reshape(-1)
    slot_tok = jnp.zeros((n_slots,), jnp.int32).at[dest_kmajor].set(token_of, unique_indices=True)
    ys = _experts(slot_tok, block_expert, x2, p["g_moe"], p["w_gu"], p["w_d"])
    y = _combine(dest_kmajor, x2, gate.reshape(t, 2), p["g_final"], ys)
    return y.reshape(b, seq_len, d)


def kernel(x_prompt, x_sample, mem_prompt, mem_sample, g_mix, w_in, rpb, w_pool, pool_scale, g_branch_a, g_branch_b, w_out, g_cross, g_mem, w_q_mem, w_kv_mem, w_o_mem, g_moe, w_coarse, b_coarse, w_fine, b_fine, w_gate_e, w_up_e, w_down_e, g_final):
    depth = w_in.shape[0]
    xp, xs = x_prompt, x_sample
    row = lambda a: a.reshape(1, -1).astype(F32)
    for l in range(depth):
        w_r = jnp.concatenate([w_coarse[l].T, w_fine[l].T], axis=0).astype(F32)
        w_r = jnp.pad(w_r, ((0, LANES - w_r.shape[0]), (0, 0)))
        wr_hi = w_r.astype(BF16)
        b_r = jnp.concatenate([b_coarse[l], b_fine[l]]).astype(F32)
        p = {
            "g_mix": row(g_mix[l]), "w_in": w_in[l].astype(BF16),
            "na_bias": _na_bias_table(rpb[l]), "g_branch_a": row(g_branch_a[l]),
            "w_pool": w_pool[l].astype(BF16), "pool_scale": row(pool_scale[l]), "g_branch_b": row(g_branch_b[l]),
            "w_out": w_out[l].astype(BF16), "g_cross": row(g_cross[l]), "g_mem": row(g_mem[l]),
            "w_q": w_q_mem[l].astype(BF16), "w_kv": w_kv_mem[l].astype(BF16), "w_o": w_o_mem[l].astype(BF16),
            "g_moe": row(g_moe[l]),
            "wr_hi": wr_hi, "wr_lo": (w_r - wr_hi.astype(F32)).astype(BF16),
            "b_r": jnp.pad(b_r, (0, LANES - b_r.shape[0])).reshape(1, LANES),
            "w_gu": jnp.concatenate([w_gate_e[l], w_up_e[l]], axis=-1).astype(BF16),
            "w_d": w_down_e[l].astype(BF16),
            "g_final": row(g_final),
        }
        assert depth == 1, "final-norm fusion assumes a single layer"
        xp = _encode(xp, mem_prompt, p)
        xs = _encode(xs, mem_sample, p)
    return (xp, xs)
```

```python
import functools

import jax
import jax.numpy as jnp
from jax import lax
from jax.experimental import pallas as pl
from jax.experimental.pallas import tpu as pltpu

F32 = jnp.float32
BF16 = jnp.bfloat16

D_MODEL = 2048
GRID_W = 64
NA_HEADS = 16
NA_WIDTH = 1024
NA_HEAD_DIM = 64
WIN_H = 8
WIN_W = 16
POOL_WINDOWS = (2, 4, 8, 16)
POOL_WIDTH = 1024
POOL_GROUP_DIM = 256
POOL_HALO = 16
N_MEM = 256
MEM_HEADS = 4
MEM_HEAD_DIM = 128
MEM_WIDTH = 512
N_EXPERT_GROUPS = 4
EXPERTS_PER_GROUP = 8
N_EXPERTS = 32
D_EXPERT = 512
MOE_BLOCK = 256
EPS = 1e-6
NEG = -1e30
LANES = 128
ROW_GROUP = 8
KV_CHUNK_ROWS = 4
SOFTMAX_ROWS = 16
VMEM_LIMIT = 56 * 1024 * 1024


def _rms(x, g):
    return x * lax.rsqrt(jnp.mean(x * x, axis=-1, keepdims=True) + EPS) * g


def _pack_bf16_pairs(x):
    n = x.shape[1] // 2
    bits = lax.bitcast_convert_type(x, jnp.uint32)
    return (bits[:, :n] >> 16) | (bits[:, n:] & jnp.uint32(0xFFFF0000))


def _unpack_bf16_pairs(u):
    lo = lax.bitcast_convert_type(u << 16, F32)
    hi = lax.bitcast_convert_type(u & jnp.uint32(0xFFFF0000), F32)
    return lo, hi


def _const_spec(shape):
    nd = len(shape)
    return pl.BlockSpec(shape, lambda *_: (0,) * nd, pipeline_mode=pl.Buffered(1))


def _params(*sem):
    return pltpu.CompilerParams(dimension_semantics=sem, vmem_limit_bytes=VMEM_LIMIT)


def _in_proj_kernel(x_ref, g_ref, w_ref, q_ref, k_ref, v_ref, u_ref):
    h = _rms(x_ref[...], g_ref[...]).astype(BF16)
    q_ref[...] = (jnp.dot(h, w_ref[:, 0:NA_WIDTH], preferred_element_type=F32)
                  * (NA_HEAD_DIM ** -0.5)).astype(BF16)
    k_ref[...] = jnp.dot(h, w_ref[:, NA_WIDTH:2 * NA_WIDTH], preferred_element_type=F32).astype(BF16)
    v_ref[...] = jnp.dot(h, w_ref[:, 2 * NA_WIDTH:3 * NA_WIDTH], preferred_element_type=F32).astype(BF16)
    u_ref[...] = jnp.dot(h, w_ref[:, 3 * NA_WIDTH:], preferred_element_type=F32)


def _in_proj(x2d, g_mix, w_in_bf, tm=512):
    t = x2d.shape[0]
    row = lambda i: (i, 0)
    return pl.pallas_call(
        _in_proj_kernel,
        grid=(t // tm,),
        in_specs=[pl.BlockSpec((tm, D_MODEL), row), _const_spec((1, D_MODEL)),
                  _const_spec(w_in_bf.shape)],
        out_specs=[pl.BlockSpec((tm, NA_WIDTH), row)] * 4,
        out_shape=[jax.ShapeDtypeStruct((t, NA_WIDTH), BF16)] * 3
        + [jax.ShapeDtypeStruct((t, POOL_WIDTH), F32)],
        compiler_params=_params("parallel"),
        name="in_proj",
    )(x2d, g_mix, w_in_bf)


def _norm_matmul_kernel(x_ref, g_ref, w_ref, o_ref):
    h = _rms(x_ref[...], g_ref[...]).astype(BF16)
    o_ref[...] = jnp.dot(h, w_ref[...], preferred_element_type=F32).astype(o_ref.dtype)


def _norm_matmul(x2d, g, w_bf, out_dtype, tm=256):
    t, k = x2d.shape
    n = w_bf.shape[1]
    return pl.pallas_call(
        _norm_matmul_kernel,
        grid=(t // tm,),
        in_specs=[pl.BlockSpec((tm, k), lambda i: (i, 0)), _const_spec((1, k)), _const_spec((k, n))],
        out_specs=pl.BlockSpec((tm, n), lambda i: (i, 0)),
        out_shape=jax.ShapeDtypeStruct((t, n), out_dtype),
        compiler_params=_params("parallel"),
        name="norm_matmul",
    )(x2d, g, w_bf)


def _pool_kernel(prev_ref, cur_ref, next_ref, w_ref, scale_ref, g_ref, o_ref, ext_ref, mix_ref, *, seq_len, tm):
    i = pl.program_id(1)
    n_i = pl.num_programs(1)
    ext_ref[0:POOL_HALO, :] = jnp.where(i > 0, prev_ref[0], 0.0)
    ext_ref[POOL_HALO:POOL_HALO + tm, :] = cur_ref[0]
    ext_ref[POOL_HALO + tm:, :] = jnp.where(i < n_i - 1, next_ref[0], 0.0)
    n_ext = tm + 2 * POOL_HALO
    pos = i * tm + lax.broadcasted_iota(jnp.int32, (tm, 1), 0)
    for gi, w in enumerate(POOL_WINDOWS):
        cols = slice(gi * POOL_GROUP_DIM, (gi + 1) * POOL_GROUP_DIM)
        e = ext_ref[:, cols]
        s = e + pltpu.roll(e, 1, 0)
        half = 1
        while 2 * half < w:
            s = pltpu.roll(s, half, 0) + pltpu.roll(s, n_ext - half, 0)
            half *= 2
        lo = jnp.maximum(pos - w // 2, 0)
        hi = jnp.minimum(pos - w // 2 + w, seq_len)
        cnt = (hi - lo).astype(F32)
        pooled = s[POOL_HALO:POOL_HALO + tm] / cnt - e[POOL_HALO:POOL_HALO + tm]
        mix_ref[:, cols] = jnp.dot(pooled.astype(BF16), w_ref[gi], preferred_element_type=F32)
    mixed = mix_ref[...] * scale_ref[...]
    o_ref[0] = _rms(mixed, g_ref[...]).astype(BF16)


def _pool_mixer(u3d, w_pool_bf, pool_scale, g_b, tm=512):
    b, seq_len, _ = u3d.shape
    tm = min(tm, seq_len)
    hb = tm // POOL_HALO
    n_hb = seq_len // POOL_HALO
    kern = functools.partial(_pool_kernel, seq_len=seq_len, tm=tm)
    return pl.pallas_call(
        kern,
        grid=(b, seq_len // tm),
        in_specs=[
            pl.BlockSpec((1, POOL_HALO, POOL_WIDTH), lambda bi, i: (bi, jnp.maximum(i * hb - 1, 0), 0)),
            pl.BlockSpec((1, tm, POOL_WIDTH), lambda bi, i: (bi, i, 0)),
            pl.BlockSpec((1, POOL_HALO, POOL_WIDTH), lambda bi, i: (bi, jnp.minimum((i + 1) * hb, n_hb - 1), 0)),
            _const_spec(w_pool_bf.shape), _const_spec((1, POOL_WIDTH)), _const_spec((1, POOL_WIDTH)),
        ],
        out_specs=pl.BlockSpec((1, tm, POOL_WIDTH), lambda bi, i: (bi, i, 0)),
        out_shape=jax.ShapeDtypeStruct((b, seq_len, POOL_WIDTH), BF16),
        scratch_shapes=[pltpu.VMEM((tm + 2 * POOL_HALO, POOL_WIDTH), F32), pltpu.VMEM((tm, POOL_WIDTH), F32)],
        compiler_params=_params("parallel", "parallel"),
        name="pool_mixer",
    )(u3d, u3d, u3d, w_pool_bf, pool_scale, g_b)


def _na_bias_table(rpb):
    c = jnp.arange(GRID_W)[:, None]
    kc = jnp.arange(GRID_W)[None, :]
    start = jnp.clip(c - WIN_W // 2, 0, GRID_W - WIN_W)
    valid = (kc >= start) & (kc < start + WIN_W)
    dc = kc - c + (WIN_W - 1)
    onehot = (dc[:, :, None] == jnp.arange(2 * WIN_W - 1)[None, None, :]).astype(F32)
    band = jnp.einsum("hrd,cqd->hrcq", rpb.astype(F32), onehot, precision=lax.Precision.HIGHEST)
    band = jnp.where(valid[None, None], band, NEG)
    pats = []
    for pat in range(WIN_H):
        win = band[:, WIN_H - 1 - pat:2 * WIN_H - 1 - pat]
        pats.append(jnp.transpose(win, (0, 2, 1, 3)).reshape(NA_HEADS // 2, 2 * GRID_W, WIN_H * GRID_W))
    return jnp.stack(pats)


def _na_kernel(q_ref, k0, k1, k2, k3, v0, v1, v2, v3, bias_ref, g_ref, o_ref, kbuf, vbuf, acc, s_scr, p_scr, *, n_rows):
    g = pl.program_id(1)
    chunk = KV_CHUNK_ROWS * GRID_W
    for c, (kr, vr) in enumerate(((k0, v0), (k1, v1), (k2, v2), (k3, v3))):
        kbuf[c * chunk:(c + 1) * chunk, :] = kr[0]
        vbuf[c * chunk:(c + 1) * chunk, :] = vr[0]
    first_head = lax.broadcasted_iota(jnp.int32, (GRID_W, LANES), 1) < NA_HEAD_DIM
    n_keys = WIN_H * GRID_W

    def row_body(j, carry):
        r = g * ROW_GROUP + j
        r0 = jnp.clip(r - WIN_H // 2, 0, n_rows - WIN_H)
        pat = r - r0
        koff = pl.multiple_of((r0 - (g * ROW_GROUP - KV_CHUNK_ROWS)) * GRID_W, GRID_W)
        qoff = pl.multiple_of(j * GRID_W, GRID_W)
        n_pairs = NA_HEADS // 2
        for hp in range(n_pairs):
            cols = slice(hp * LANES, (hp + 1) * LANES)
            q2 = q_ref[0, pl.ds(qoff, GRID_W), cols]
            zero = jnp.zeros_like(q2)
            qq = jnp.concatenate([jnp.where(first_head, q2, zero), jnp.where(first_head, zero, q2)], axis=0)
            kk = kbuf[pl.ds(koff, n_keys), cols]
            s_scr[hp] = lax.dot_general(qq, kk, (((1,), (1,)), ((), ())), preferred_element_type=F32)
        for hp in range(n_pairs):
            for rc in range(2 * GRID_W // SOFTMAX_ROWS):
                rows = slice(rc * SOFTMAX_ROWS, (rc + 1) * SOFTMAX_ROWS)
                s = s_scr[hp, rows, :] + bias_ref[pat, hp, rows, :]
                p = jnp.exp(s - jnp.max(s, axis=-1, keepdims=True))
                p = p * (1.0 / jnp.sum(p, axis=-1, keepdims=True))
                p_scr[hp, rows, :] = p.astype(BF16)
        for hp in range(n_pairs):
            cols = slice(hp * LANES, (hp + 1) * LANES)
            vv = vbuf[pl.ds(koff, n_keys), cols]
            o = jnp.dot(p_scr[hp], vv, preferred_element_type=F32)
            acc[pl.ds(qoff, GRID_W), cols] = jnp.where(first_head, o[:GRID_W], o[GRID_W:])
        return carry

    lax.fori_loop(0, ROW_GROUP, row_body, 0)
    o_ref[0] = _rms(acc[...], g_ref[...]).astype(BF16)


def _neighborhood_attention(q3d, k3d, v3d, bias_tbl, g_a):
    b, seq_len, _ = q3d.shape
    n_rows = seq_len // GRID_W
    tq = ROW_GROUP * GRID_W
    chunk = KV_CHUNK_ROWS * GRID_W
    n_chunks = seq_len // chunk

    def kv_spec(c):
        return pl.BlockSpec((1, chunk, NA_WIDTH),
                            lambda bi, g: (bi, jnp.clip(2 * g - 1 + c, 0, n_chunks - 1), 0))

    kern = functools.partial(_na_kernel, n_rows=n_rows)
    return pl.pallas_call(
        kern,
        grid=(b, n_rows // ROW_GROUP),
        in_specs=[pl.BlockSpec((1, tq, NA_WIDTH), lambda bi, g: (bi, g, 0))]
        + [kv_spec(c) for c in range(4)] * 2
        + [_const_spec(bias_tbl.shape), _const_spec((1, NA_WIDTH))],
        out_specs=pl.BlockSpec((1, tq, NA_WIDTH), lambda bi, g: (bi, g, 0)),
        out_shape=jax.ShapeDtypeStruct((b, seq_len, NA_WIDTH), BF16),
        scratch_shapes=[pltpu.VMEM((4 * chunk, NA_WIDTH), BF16), pltpu.VMEM((4 * chunk, NA_WIDTH), BF16),
                        pltpu.VMEM((tq, NA_WIDTH), F32),
                        pltpu.VMEM((NA_HEADS // 2, 2 * GRID_W, WIN_H * GRID_W), F32),
                        pltpu.VMEM((NA_HEADS // 2, 2 * GRID_W, WIN_H * GRID_W), BF16)],
        compiler_params=_params("parallel", "parallel"),
        name="neighborhood_attention",
    )(q3d, k3d, k3d, k3d, k3d, v3d, v3d, v3d, v3d, bias_tbl, g_a)


def _post_kernel(x_ref, a_ref, p_ref, wout_ref, gc_ref, wq_ref, km_ref, vm_ref, wo_ref, gm_ref,
                 wr_hi_ref, wr_lo_ref, br_ref, x2_ref, h3p_ref, eid_ref, gate_ref):
    tm = x_ref.shape[1]
    x1 = (x_ref[0]
          + jnp.dot(a_ref[0], wout_ref[0:NA_WIDTH, :], preferred_element_type=F32)
          + jnp.dot(p_ref[0], wout_ref[NA_WIDTH:, :], preferred_element_type=F32))
    h2 = _rms(x1, gc_ref[...]).astype(BF16)
    qm = jnp.dot(h2, wq_ref[...], preferred_element_type=F32).astype(BF16)
    heads = []
    for hd in range(MEM_HEADS):
        cols = slice(hd * MEM_HEAD_DIM, (hd + 1) * MEM_HEAD_DIM)
        s = lax.dot_general(qm[:, cols], km_ref[0, :, cols], (((1,), (1,)), ((), ())),
                            preferred_element_type=F32) * (MEM_HEAD_DIM ** -0.5)
        m = jnp.max(s, axis=-1, keepdims=True)
        p = jnp.exp(s - m)
        l = jnp.sum(p, axis=-1, keepdims=True)
        o = jnp.dot(p.astype(BF16), vm_ref[0, :, cols], preferred_element_type=F32) / l
        heads.append(o.astype(BF16))
    o_all = jnp.concatenate(heads, axis=-1)
    x2 = x1 + jnp.dot(o_all, wo_ref[...], preferred_element_type=F32)
    x2_ref[0] = x2
    h3 = _rms(x2, gm_ref[...])

    h_hi = h3.astype(BF16)
    h_lo = (h3 - h_hi.astype(F32)).astype(BF16)
    h3p_ref[0] = _pack_bf16_pairs(h_hi.astype(F32))
    nt = (((1,), (1,)), ((), ()))
    logits = (lax.dot_general(h_hi, wr_hi_ref[...], nt, preferred_element_type=F32)
              + lax.dot_general(h_lo, wr_hi_ref[...], nt, preferred_element_type=F32)
              + lax.dot_general(h_hi, wr_lo_ref[...], nt, preferred_element_type=F32)) + br_ref[...]

    lane = lax.broadcasted_iota(jnp.int32, (tm, LANES), 1)
    lane_f = lane.astype(F32)
    big = float(LANES)
    cmask = lane < N_EXPERT_GROUPS
    lc = jnp.where(cmask, logits, NEG)
    cmax = jnp.max(lc, axis=-1, keepdims=True)
    g_idx = jnp.min(jnp.where(lc == cmax, lane_f, big), axis=-1, keepdims=True)
    g_gate = 1.0 / jnp.sum(jnp.exp(lc - cmax), axis=-1, keepdims=True)
    grp_f = ((lane - N_EXPERT_GROUPS) >> 3).astype(F32)
    in_fine = jnp.where(lane >= N_EXPERT_GROUPS, jnp.where(lane < N_EXPERT_GROUPS + N_EXPERTS, grp_f, -1.0), -1.0)
    lf = jnp.where(in_fine == g_idx, logits, NEG)
    v1 = jnp.max(lf, axis=-1, keepdims=True)
    i1 = jnp.min(jnp.where(lf == v1, lane_f, big), axis=-1, keepdims=True)
    lf2 = jnp.where(lane_f == i1, NEG, lf)
    v2 = jnp.max(lf2, axis=-1, keepdims=True)
    i2 = jnp.min(jnp.where(lf2 == v2, lane_f, big), axis=-1, keepdims=True)
    e21 = jnp.exp(v2 - v1)
    den = 1.0 / (1.0 + e21)
    two = lax.broadcasted_iota(jnp.int32, (tm, 2), 1)
    eid_ref[0] = jnp.where(two == 0, i1, i2).astype(jnp.int32) - N_EXPERT_GROUPS
    gate_ref[0] = jnp.where(two == 0, g_gate * den, g_gate * (e21 * den))


def _post_mix(x3d, a_n, p_n, w_out_bf, g_cross, w_q_bf, k_mem, v_mem, w_o_bf, g_moe, wr_hi, wr_lo, b_r, tm=256):
    b, seq_len, _ = x3d.shape
    tok = lambda w: pl.BlockSpec((1, tm, w), lambda bi, i: (bi, i, 0))
    mem = pl.BlockSpec((1, N_MEM, MEM_WIDTH), lambda bi, i: (bi, 0, 0))
    return pl.pallas_call(
        _post_kernel,
        grid=(b, seq_len // tm),
        in_specs=[tok(D_MODEL), tok(NA_WIDTH), tok(POOL_WIDTH), _const_spec(w_out_bf.shape),
                  _const_spec((1, D_MODEL)), _const_spec(w_q_bf.shape), mem, mem, _const_spec(w_o_bf.shape),
                  _const_spec((1, D_MODEL)), _const_spec(wr_hi.shape), _const_spec(wr_lo.shape),
                  _const_spec((1, LANES))],
        out_specs=[tok(D_MODEL), tok(D_MODEL // 2), tok(2), tok(2)],
        out_shape=[jax.ShapeDtypeStruct((b, seq_len, D_MODEL), F32),
                   jax.ShapeDtypeStruct((b, seq_len, D_MODEL // 2), jnp.uint32),
                   jax.ShapeDtypeStruct((b, seq_len, 2), jnp.int32), jax.ShapeDtypeStruct((b, seq_len, 2), F32)],
        compiler_params=_params("parallel", "parallel"),
        name="post_mix",
    )(x3d, a_n, p_n, w_out_bf, g_cross, w_q_bf, k_mem, v_mem, w_o_bf, g_moe, wr_hi, wr_lo, b_r)


def _rank_kernel(eid_ref, rank_ref, cnt_ref, carry):
    tm = eid_ref.shape[0]

    @pl.when(pl.program_id(0) == 0)
    def _():
        carry[...] = jnp.zeros_like(carry)

    lane = lax.broadcasted_iota(jnp.int32, (tm, LANES), 1)
    e1 = eid_ref[:, 0:1]
    e2 = eid_ref[:, 1:2]
    hit1 = lane == e1
    hit2 = lane == e2
    onehot = jnp.where(hit1, 1.0, 0.0) + jnp.where(hit2, 1.0, 0.0)
    row = lax.broadcasted_iota(jnp.int32, (tm, tm), 0)
    col = lax.broadcasted_iota(jnp.int32, (tm, tm), 1)
    earlier = jnp.where(col < row, 1.0, 0.0).astype(BF16)
    before = jnp.dot(earlier, onehot.astype(BF16), preferred_element_type=F32) + carry[...]
    r1 = jnp.sum(jnp.where(hit1, before, 0.0), axis=-1, keepdims=True)
    r2 = jnp.sum(jnp.where(hit2, before, 0.0), axis=-1, keepdims=True)
    two = lax.broadcasted_iota(jnp.int32, (tm, 2), 1)
    rank_ref[...] = jnp.where(two == 0, r1, r2).astype(jnp.int32)
    total = carry[...] + jnp.sum(onehot, axis=0, keepdims=True)
    carry[...] = total
    cnt_ref[...] = total.astype(jnp.int32)


def _expert_ranks(eid2d, tm=512):
    t = eid2d.shape[0]
    return pl.pallas_call(
        _rank_kernel,
        grid=(t // tm,),
        in_specs=[pl.BlockSpec((tm, 2), lambda i: (i, 0))],
        out_specs=[pl.BlockSpec((tm, 2), lambda i: (i, 0)), pl.BlockSpec((1, LANES), lambda i: (0, 0))],
        out_shape=[jax.ShapeDtypeStruct((t, 2), jnp.int32), jax.ShapeDtypeStruct((1, LANES), jnp.int32)],
        scratch_shapes=[pltpu.VMEM((1, LANES), F32)],
        compiler_params=_params("arbitrary"),
        name="expert_ranks",
    )(eid2d)


def _slot_tok_kernel(dest_ref, out_ref, *, n_tok, n_slots):
    unroll = 8

    def clear(c, carry):
        for u in range(unroll):
            out_ref[c * unroll + u] = 0
        return carry

    lax.fori_loop(0, n_slots // unroll, clear, 0)

    for k in range(2):
        def put(c, carry, k=k):
            for u in range(unroll):
                tok = c * unroll + u
                out_ref[dest_ref[k * n_tok + tok]] = tok
            return carry

        lax.fori_loop(0, n_tok // unroll, put, 0)


def _slot_tokens(dest_kmajor, n_slots):
    n_tok = dest_kmajor.shape[0] // 2
    kern = functools.partial(_slot_tok_kernel, n_tok=n_tok, n_slots=n_slots)
    return pl.pallas_call(
        kern,
        grid_spec=pltpu.PrefetchScalarGridSpec(
            num_scalar_prefetch=1,
            grid=(1,),
            in_specs=[],
            out_specs=pl.BlockSpec(memory_space=pltpu.SMEM),
        ),
        out_shape=jax.ShapeDtypeStruct((n_slots,), jnp.int32),
        name="moe_slot_tokens",
    )(dest_kmajor)


def _row_gather(src_ref, idx_ref, idx_base, n_rows, dst_ref, sem):
    for r in range(n_rows):
        pltpu.make_async_copy(src_ref.at[pl.ds(idx_ref[idx_base + r], 1)], dst_ref.at[pl.ds(r, 1)], sem).start()


def _row_gather_wait(src_ref, n_rows, dst_ref, sem):
    for r in range(n_rows):
        pltpu.make_async_copy(src_ref.at[pl.ds(0, 1)], dst_ref.at[pl.ds(r, 1)], sem).wait()


def _expert_kernel(tok_ref, be_ref, hp_ref, wgu_ref, wd_ref, ys_ref, xbuf, sems):
    del be_ref
    i = pl.program_id(0)
    n = pl.num_programs(0)
    slot = i % 2

    unroll = 16

    def start_block(block, buf_slot):
        def issue(c, carry):
            _row_gather(hp_ref, tok_ref, block * MOE_BLOCK + c * unroll, unroll,
                        xbuf.at[buf_slot, pl.ds(c * unroll, unroll)], sems.at[buf_slot])
            return carry

        lax.fori_loop(0, MOE_BLOCK // unroll, issue, 0)

    @pl.when(i == 0)
    def _():
        start_block(0, 0)

    start_block(jnp.minimum(i + 1, n - 1), 1 - slot)
    _row_gather_wait(hp_ref, MOE_BLOCK, xbuf.at[slot], sems.at[slot])
    x_lo, x_hi = _unpack_bf16_pairs(xbuf[slot])
    half = x_lo.shape[1]
    gu = (jnp.dot(x_lo.astype(BF16), wgu_ref[0, :half, :], preferred_element_type=F32)
          + jnp.dot(x_hi.astype(BF16), wgu_ref[0, half:, :], preferred_element_type=F32))
    gate = gu[:, :D_EXPERT]
    hid = gate * jax.nn.sigmoid(gate) * gu[:, D_EXPERT:]
    y = jnp.dot(hid.astype(BF16), wd_ref[0], preferred_element_type=F32)
    ys_ref[...] = _pack_bf16_pairs(y.astype(BF16).astype(F32))

    @pl.when(i == n - 1)
    def _():
        _row_gather_wait(hp_ref, MOE_BLOCK, xbuf.at[1 - slot], sems.at[1 - slot])


def _experts(slot_tok, block_expert, h3p_2d, w_gu_bf, w_d_bf):
    n_slots = slot_tok.shape[0]
    half = h3p_2d.shape[1]
    d = 2 * half
    return pl.pallas_call(
        _expert_kernel,
        grid_spec=pltpu.PrefetchScalarGridSpec(
            num_scalar_prefetch=2,
            grid=(n_slots // MOE_BLOCK,),
            in_specs=[pl.BlockSpec(memory_space=pl.ANY),
                      pl.BlockSpec((1, d, 2 * D_EXPERT), lambda i, tok, be: (be[i], 0, 0)),
                      pl.BlockSpec((1, D_EXPERT, d), lambda i, tok, be: (be[i], 0, 0))],
            out_specs=pl.BlockSpec((MOE_BLOCK, half), lambda i, tok, be: (i, 0)),
            scratch_shapes=[pltpu.VMEM((2, MOE_BLOCK, half), jnp.uint32), pltpu.SemaphoreType.DMA((2,))],
        ),
        out_shape=jax.ShapeDtypeStruct((n_slots, half), jnp.uint32),
        compiler_params=_params("arbitrary"),
        name="moe_experts",
    )(slot_tok, block_expert, h3p_2d, w_gu_bf, w_d_bf)


def _combine_kernel(dest_ref, x2_ref, gate_ref, g_ref, ys_ref, o_ref, ybuf, sems, *, tm):
    i = pl.program_id(0)
    n = pl.num_programs(0)
    slot = i % 2
    unroll = 8

    n_tok = n * tm

    def start_tile(tile, buf_slot):
        def issue(c, carry):
            for k in range(2):
                _row_gather(ys_ref, dest_ref, k * n_tok + tile * tm + c * unroll, unroll,
                            ybuf.at[buf_slot, k, pl.ds(c * unroll, unroll)], sems.at[buf_slot])
            return carry

        lax.fori_loop(0, tm // unroll, issue, 0)

    @pl.when(i == 0)
    def _():
        start_tile(0, 0)

    @pl.when(i + 1 < n)
    def _():
        start_tile(i + 1, 1 - slot)

    def drain(c, carry):
        for k in range(2):
            _row_gather_wait(ys_ref, unroll, ybuf.at[slot, k, pl.ds(c * unroll, unroll)], sems.at[slot])
        return carry

    lax.fori_loop(0, tm // unroll, drain, 0)
    gates = gate_ref[...]
    lo0, hi0 = _unpack_bf16_pairs(ybuf[slot, 0])
    lo1, hi1 = _unpack_bf16_pairs(ybuf[slot, 1])
    y = jnp.concatenate([gates[:, 0:1] * lo0 + gates[:, 1:2] * lo1,
                         gates[:, 0:1] * hi0 + gates[:, 1:2] * hi1], axis=-1)
    o_ref[...] = _rms(x2_ref[...] + y, g_ref[...])


def _combine(dest_flat, x2_2d, gate2d, g_final, ys, tm=256):
    t, d = x2_2d.shape
    kern = functools.partial(_combine_kernel, tm=tm)
    return pl.pallas_call(
        kern,
        grid_spec=pltpu.PrefetchScalarGridSpec(
            num_scalar_prefetch=1,
            grid=(t // tm,),
            in_specs=[pl.BlockSpec((tm, d), lambda i, dst: (i, 0)),
                      pl.BlockSpec((tm, 2), lambda i, dst: (i, 0)),
                      pl.BlockSpec((1, d), lambda i, dst: (0, 0)),
                      pl.BlockSpec(memory_space=pl.ANY)],
            out_specs=pl.BlockSpec((tm, d), lambda i, dst: (i, 0)),
            scratch_shapes=[pltpu.VMEM((2, 2, tm, d // 2), jnp.uint32), pltpu.SemaphoreType.DMA((2,))],
        ),
        out_shape=jax.ShapeDtypeStruct((t, d), F32),
        compiler_params=_params("arbitrary"),
        name="moe_combine",
    )(dest_flat, x2_2d, gate2d, g_final, ys)


def _encode(x, mem, p):
    b, seq_len, d = x.shape
    t = b * seq_len
    q, k, v, u = _in_proj(x.reshape(t, d), p["g_mix"], p["w_in"])
    shape3 = (b, seq_len, NA_WIDTH)
    a_n = _neighborhood_attention(q.reshape(shape3), k.reshape(shape3), v.reshape(shape3), p["na_bias"], p["g_branch_a"])
    p_n = _pool_mixer(u.reshape(shape3), p["w_pool"], p["pool_scale"], p["g_branch_b"])
    kv = _norm_matmul(mem.reshape(b * N_MEM, d), p["g_mem"], p["w_kv"], BF16).reshape(b, N_MEM, 2 * MEM_WIDTH)
    x2, h3p, eid, gate = _post_mix(x, a_n, p_n, p["w_out"], p["g_cross"], p["w_q"], kv[:, :, :MEM_WIDTH],
                                   kv[:, :, MEM_WIDTH:], p["w_o"], p["g_moe"], p["wr_hi"], p["wr_lo"], p["b_r"])
    x2 = x2.reshape(t, d)
    eid2d = eid.reshape(t, 2)
    rank, counts = _expert_ranks(eid2d)
    counts = counts[0, :N_EXPERTS]
    padded = ((counts + MOE_BLOCK - 1) // MOE_BLOCK) * MOE_BLOCK
    padded_ends = jnp.cumsum(padded)
    padded_starts = padded_ends - padded
    n_blocks = -(-(2 * t + N_EXPERTS * (MOE_BLOCK - 1)) // MOE_BLOCK)
    n_slots = n_blocks * MOE_BLOCK
    block_pos = jnp.arange(n_blocks, dtype=jnp.int32) * MOE_BLOCK
    block_expert = jnp.minimum(jnp.sum(padded_ends[None, :] <= block_pos[:, None], axis=1), N_EXPERTS - 1).astype(jnp.int32)
    is_e = eid2d[:, :, None] == jnp.arange(N_EXPERTS, dtype=jnp.int32)
    dest = (jnp.sum(jnp.where(is_e, padded_starts, 0), axis=-1) + rank).astype(jnp.int32)
    dest_kmajor = dest.T.reshape(-1)
    slot_tok = _slot_tokens(dest_kmajor, n_slots)
    ys = _experts(slot_tok, block_expert, h3p.reshape(t, d // 2), p["w_gu"], p["w_d"])
    y = _combine(dest_kmajor, x2, gate.reshape(t, 2), p["g_final"], ys)
    return y.reshape(b, seq_len, d)


def kernel(x_prompt, x_sample, mem_prompt, mem_sample, g_mix, w_in, rpb, w_pool, pool_scale, g_branch_a, g_branch_b, w_out, g_cross, g_mem, w_q_mem, w_kv_mem, w_o_mem, g_moe, w_coarse, b_coarse, w_fine, b_fine, w_gate_e, w_up_e, w_down_e, g_final):
    depth = w_in.shape[0]
    xp, xs = x_prompt, x_sample
    row = lambda a: a.reshape(1, -1).astype(F32)
    for l in range(depth):
        w_r = jnp.concatenate([w_coarse[l].T, w_fine[l].T], axis=0).astype(F32)
        w_r = jnp.pad(w_r, ((0, LANES - w_r.shape[0]), (0, 0)))
        wr_hi = w_r.astype(BF16)
        b_r = jnp.concatenate([b_coarse[l], b_fine[l]]).astype(F32)
        p = {
            "g_mix": row(g_mix[l]), "w_in": w_in[l].astype(BF16),
            "na_bias": _na_bias_table(rpb[l]), "g_branch_a": row(g_branch_a[l]),
            "w_pool": w_pool[l].astype(BF16), "pool_scale": row(pool_scale[l]), "g_branch_b": row(g_branch_b[l]),
            "w_out": w_out[l].astype(BF16), "g_cross": row(g_cross[l]), "g_mem": row(g_mem[l]),
            "w_q": w_q_mem[l].astype(BF16), "w_kv": w_kv_mem[l].astype(BF16), "w_o": w_o_mem[l].astype(BF16),
            "g_moe": row(g_moe[l]),
            "wr_hi": wr_hi, "wr_lo": (w_r - wr_hi.astype(F32)).astype(BF16),
            "b_r": jnp.pad(b_r, (0, LANES - b_r.shape[0])).reshape(1, LANES),
            "w_gu": jnp.concatenate([w_gate_e[l], w_up_e[l]], axis=-1).astype(BF16),
            "w_d": w_down_e[l].astype(BF16),
            "g_final": row(g_final),
        }
        assert depth == 1, "final-norm fusion assumes a single layer"
        xp = _encode(xp, mem_prompt, p)
        xs = _encode(xs, mem_sample, p)
    return (xp, xs)
```

```python
import functools

import jax
import jax.numpy as jnp
from jax import lax
from jax.experimental import pallas as pl
from jax.experimental.pallas import tpu as pltpu

F32 = jnp.float32
BF16 = jnp.bfloat16

D_MODEL = 2048
GRID_W = 64
NA_HEADS = 16
NA_WIDTH = 1024
NA_HEAD_DIM = 64
WIN_H = 8
WIN_W = 16
POOL_WINDOWS = (2, 4, 8, 16)
POOL_WIDTH = 1024
POOL_GROUP_DIM = 256
POOL_HALO = 16
N_MEM = 256
MEM_HEADS = 4
MEM_HEAD_DIM = 128
MEM_WIDTH = 512
N_EXPERT_GROUPS = 4
EXPERTS_PER_GROUP = 8
N_EXPERTS = 32
D_EXPERT = 512
MOE_BLOCK = 256
EPS = 1e-6
NEG = -1e30
LANES = 128
ROW_GROUP = 8
KV_CHUNK_ROWS = 4
SOFTMAX_ROWS = 16
VMEM_LIMIT = 56 * 1024 * 1024


def _rms(x, g):
    return x * lax.rsqrt(jnp.mean(x * x, axis=-1, keepdims=True) + EPS) * g


def _pack_bf16_pairs(x):
    n = x.shape[1] // 2
    bits = lax.bitcast_convert_type(x, jnp.uint32)
    return (bits[:, :n] >> 16) | (bits[:, n:] & jnp.uint32(0xFFFF0000))


def _unpack_bf16_pairs(u):
    lo = lax.bitcast_convert_type(u << 16, F32)
    hi = lax.bitcast_convert_type(u & jnp.uint32(0xFFFF0000), F32)
    return lo, hi


def _const_spec(shape):
    nd = len(shape)
    return pl.BlockSpec(shape, lambda *_: (0,) * nd, pipeline_mode=pl.Buffered(1))


def _params(*sem):
    return pltpu.CompilerParams(dimension_semantics=sem, vmem_limit_bytes=VMEM_LIMIT)


def _in_proj_kernel(x_ref, g_ref, w_ref, q_ref, k_ref, v_ref, u_ref):
    h = _rms(x_ref[...], g_ref[...]).astype(BF16)
    q_ref[...] = (jnp.dot(h, w_ref[:, 0:NA_WIDTH], preferred_element_type=F32)
                  * (NA_HEAD_DIM ** -0.5)).astype(BF16)
    k_ref[...] = jnp.dot(h, w_ref[:, NA_WIDTH:2 * NA_WIDTH], preferred_element_type=F32).astype(BF16)
    v_ref[...] = jnp.dot(h, w_ref[:, 2 * NA_WIDTH:3 * NA_WIDTH], preferred_element_type=F32).astype(BF16)
    u_ref[...] = jnp.dot(h, w_ref[:, 3 * NA_WIDTH:], preferred_element_type=F32)


def _in_proj(x2d, g_mix, w_in_bf, tm=512):
    t = x2d.shape[0]
    row = lambda i: (i, 0)
    return pl.pallas_call(
        _in_proj_kernel,
        grid=(t // tm,),
        in_specs=[pl.BlockSpec((tm, D_MODEL), row), _const_spec((1, D_MODEL)),
                  _const_spec(w_in_bf.shape)],
        out_specs=[pl.BlockSpec((tm, NA_WIDTH), row)] * 4,
        out_shape=[jax.ShapeDtypeStruct((t, NA_WIDTH), BF16)] * 3
        + [jax.ShapeDtypeStruct((t, POOL_WIDTH), F32)],
        compiler_params=_params("parallel"),
        name="in_proj",
    )(x2d, g_mix, w_in_bf)


def _norm_matmul_kernel(x_ref, g_ref, w_ref, o_ref):
    h = _rms(x_ref[...], g_ref[...]).astype(BF16)
    o_ref[...] = jnp.dot(h, w_ref[...], preferred_element_type=F32).astype(o_ref.dtype)


def _norm_matmul(x2d, g, w_bf, out_dtype, tm=256):
    t, k = x2d.shape
    n = w_bf.shape[1]
    return pl.pallas_call(
        _norm_matmul_kernel,
        grid=(t // tm,),
        in_specs=[pl.BlockSpec((tm, k), lambda i: (i, 0)), _const_spec((1, k)), _const_spec((k, n))],
        out_specs=pl.BlockSpec((tm, n), lambda i: (i, 0)),
        out_shape=jax.ShapeDtypeStruct((t, n), out_dtype),
        compiler_params=_params("parallel"),
        name="norm_matmul",
    )(x2d, g, w_bf)


def _pool_kernel(prev_ref, cur_ref, next_ref, w_ref, scale_ref, g_ref, o_ref, ext_ref, mix_ref, *, seq_len, tm):
    i = pl.program_id(1)
    n_i = pl.num_programs(1)
    ext_ref[0:POOL_HALO, :] = jnp.where(i > 0, prev_ref[0], 0.0)
    ext_ref[POOL_HALO:POOL_HALO + tm, :] = cur_ref[0]
    ext_ref[POOL_HALO + tm:, :] = jnp.where(i < n_i - 1, next_ref[0], 0.0)
    n_ext = tm + 2 * POOL_HALO
    pos = i * tm + lax.broadcasted_iota(jnp.int32, (tm, 1), 0)
    for gi, w in enumerate(POOL_WINDOWS):
        cols = slice(gi * POOL_GROUP_DIM, (gi + 1) * POOL_GROUP_DIM)
        e = ext_ref[:, cols]
        s = e + pltpu.roll(e, 1, 0)
        half = 1
        while 2 * half < w:
            s = pltpu.roll(s, half, 0) + pltpu.roll(s, n_ext - half, 0)
            half *= 2
        lo = jnp.maximum(pos - w // 2, 0)
        hi = jnp.minimum(pos - w // 2 + w, seq_len)
        cnt = (hi - lo).astype(F32)
        pooled = s[POOL_HALO:POOL_HALO + tm] / cnt - e[POOL_HALO:POOL_HALO + tm]
        mix_ref[:, cols] = jnp.dot(pooled.astype(BF16), w_ref[gi], preferred_element_type=F32)
    mixed = mix_ref[...] * scale_ref[...]
    o_ref[0] = _rms(mixed, g_ref[...]).astype(BF16)


def _pool_mixer(u3d, w_pool_bf, pool_scale, g_b, tm=512):
    b, seq_len, _ = u3d.shape
    tm = min(tm, seq_len)
    hb = tm // POOL_HALO
    n_hb = seq_len // POOL_HALO
    kern = functools.partial(_pool_kernel, seq_len=seq_len, tm=tm)
    return pl.pallas_call(
        kern,
        grid=(b, seq_len // tm),
        in_specs=[
            pl.BlockSpec((1, POOL_HALO, POOL_WIDTH), lambda bi, i: (bi, jnp.maximum(i * hb - 1, 0), 0)),
            pl.BlockSpec((1, tm, POOL_WIDTH), lambda bi, i: (bi, i, 0)),
            pl.BlockSpec((1, POOL_HALO, POOL_WIDTH), lambda bi, i: (bi, jnp.minimum((i + 1) * hb, n_hb - 1), 0)),
            _const_spec(w_pool_bf.shape), _const_spec((1, POOL_WIDTH)), _const_spec((1, POOL_WIDTH)),
        ],
        out_specs=pl.BlockSpec((1, tm, POOL_WIDTH), lambda bi, i: (bi, i, 0)),
        out_shape=jax.ShapeDtypeStruct((b, seq_len, POOL_WIDTH), BF16),
        scratch_shapes=[pltpu.VMEM((tm + 2 * POOL_HALO, POOL_WIDTH), F32), pltpu.VMEM((tm, POOL_WIDTH), F32)],
        compiler_params=_params("parallel", "parallel"),
        name="pool_mixer",
    )(u3d, u3d, u3d, w_pool_bf, pool_scale, g_b)


def _na_bias_table(rpb):
    c = jnp.arange(GRID_W)[:, None]
    kc = jnp.arange(GRID_W)[None, :]
    start = jnp.clip(c - WIN_W // 2, 0, GRID_W - WIN_W)
    valid = (kc >= start) & (kc < start + WIN_W)
    dc = kc - c + (WIN_W - 1)
    onehot = (dc[:, :, None] == jnp.arange(2 * WIN_W - 1)[None, None, :]).astype(F32)
    band = jnp.einsum("hrd,cqd->hrcq", rpb.astype(F32), onehot, precision=lax.Precision.HIGHEST)
    band = jnp.where(valid[None, None], band, NEG)
    pats = []
    for pat in range(WIN_H):
        win = band[:, WIN_H - 1 - pat:2 * WIN_H - 1 - pat]
        pats.append(jnp.transpose(win, (0, 2, 1, 3)).reshape(NA_HEADS // 2, 2 * GRID_W, WIN_H * GRID_W))
    return jnp.stack(pats)


def _na_kernel(q_ref, k0, k1, k2, k3, v0, v1, v2, v3, bias_ref, g_ref, o_ref, kbuf, vbuf, acc, s_scr, p_scr, *, n_rows):
    g = pl.program_id(1)
    chunk = KV_CHUNK_ROWS * GRID_W
    for c, (kr, vr) in enumerate(((k0, v0), (k1, v1), (k2, v2), (k3, v3))):
        kbuf[c * chunk:(c + 1) * chunk, :] = kr[0]
        vbuf[c * chunk:(c + 1) * chunk, :] = vr[0]
    first_head = lax.broadcasted_iota(jnp.int32, (GRID_W, LANES), 1) < NA_HEAD_DIM
    n_keys = WIN_H * GRID_W

    def row_body(j, carry):
        r = g * ROW_GROUP + j
        r0 = jnp.clip(r - WIN_H // 2, 0, n_rows - WIN_H)
        pat = r - r0
        koff = pl.multiple_of((r0 - (g * ROW_GROUP - KV_CHUNK_ROWS)) * GRID_W, GRID_W)
        qoff = pl.multiple_of(j * GRID_W, GRID_W)
        n_pairs = NA_HEADS // 2
        for hp in range(n_pairs):
            cols = slice(hp * LANES, (hp + 1) * LANES)
            q2 = q_ref[0, pl.ds(qoff, GRID_W), cols]
            zero = jnp.zeros_like(q2)
            qq = jnp.concatenate([jnp.where(first_head, q2, zero), jnp.where(first_head, zero, q2)], axis=0)
            kk = kbuf[pl.ds(koff, n_keys), cols]
            s_scr[hp] = lax.dot_general(qq, kk, (((1,), (1,)), ((), ())), preferred_element_type=F32)
        for hp in range(n_pairs):
            for rc in range(2 * GRID_W // SOFTMAX_ROWS):
                rows = slice(rc * SOFTMAX_ROWS, (rc + 1) * SOFTMAX_ROWS)
                s = s_scr[hp, rows, :] + bias_ref[pat, hp, rows, :]
                p = jnp.exp(s - jnp.max(s, axis=-1, keepdims=True))
                p = p * (1.0 / jnp.sum(p, axis=-1, keepdims=True))
                p_scr[hp, rows, :] = p.astype(BF16)
        for hp in range(n_pairs):
            cols = slice(hp * LANES, (hp + 1) * LANES)
            vv = vbuf[pl.ds(koff, n_keys), cols]
            o = jnp.dot(p_scr[hp], vv, preferred_element_type=F32)
            acc[pl.ds(qoff, GRID_W), cols] = jnp.where(first_head, o[:GRID_W], o[GRID_W:])
        return carry

    lax.fori_loop(0, ROW_GROUP, row_body, 0)
    o_ref[0] = _rms(acc[...], g_ref[...]).astype(BF16)


def _neighborhood_attention(q3d, k3d, v3d, bias_tbl, g_a):
    b, seq_len, _ = q3d.shape
    n_rows = seq_len // GRID_W
    tq = ROW_GROUP * GRID_W
    chunk = KV_CHUNK_ROWS * GRID_W
    n_chunks = seq_len // chunk

    def kv_spec(c):
        return pl.BlockSpec((1, chunk, NA_WIDTH),
                            lambda bi, g: (bi, jnp.clip(2 * g - 1 + c, 0, n_chunks - 1), 0))

    kern = functools.partial(_na_kernel, n_rows=n_rows)
    return pl.pallas_call(
        kern,
        grid=(b, n_rows // ROW_GROUP),
        in_specs=[pl.BlockSpec((1, tq, NA_WIDTH), lambda bi, g: (bi, g, 0))]
        + [kv_spec(c) for c in range(4)] * 2
        + [_const_spec(bias_tbl.shape), _const_spec((1, NA_WIDTH))],
        out_specs=pl.BlockSpec((1, tq, NA_WIDTH), lambda bi, g: (bi, g, 0)),
        out_shape=jax.ShapeDtypeStruct((b, seq_len, NA_WIDTH), BF16),
        scratch_shapes=[pltpu.VMEM((4 * chunk, NA_WIDTH), BF16), pltpu.VMEM((4 * chunk, NA_WIDTH), BF16),
                        pltpu.VMEM((tq, NA_WIDTH), F32),
                        pltpu.VMEM((NA_HEADS // 2, 2 * GRID_W, WIN_H * GRID_W), F32),
                        pltpu.VMEM((NA_HEADS // 2, 2 * GRID_W, WIN_H * GRID_W), BF16)],
        compiler_params=_params("parallel", "parallel"),
        name="neighborhood_attention",
    )(q3d, k3d, k3d, k3d, k3d, v3d, v3d, v3d, v3d, bias_tbl, g_a)


def _post_kernel(x_ref, a_ref, p_ref, wout_ref, gc_ref, wq_ref, km_ref, vm_ref, wo_ref, gm_ref,
                 wr_hi_ref, wr_lo_ref, br_ref, x2_ref, h3p_ref, eid_ref, gate_ref):
    tm = x_ref.shape[1]
    x1 = (x_ref[0]
          + jnp.dot(a_ref[0], wout_ref[0:NA_WIDTH, :], preferred_element_type=F32)
          + jnp.dot(p_ref[0], wout_ref[NA_WIDTH:, :], preferred_element_type=F32))
    h2 = _rms(x1, gc_ref[...]).astype(BF16)
    qm = jnp.dot(h2, wq_ref[...], preferred_element_type=F32).astype(BF16)
    heads = []
    for hd in range(MEM_HEADS):
        cols = slice(hd * MEM_HEAD_DIM, (hd + 1) * MEM_HEAD_DIM)
        s = lax.dot_general(qm[:, cols], km_ref[0, :, cols], (((1,), (1,)), ((), ())),
                            preferred_element_type=F32) * (MEM_HEAD_DIM ** -0.5)
        m = jnp.max(s, axis=-1, keepdims=True)
        p = jnp.exp(s - m)
        l = jnp.sum(p, axis=-1, keepdims=True)
        o = jnp.dot(p.astype(BF16), vm_ref[0, :, cols], preferred_element_type=F32) / l
        heads.append(o.astype(BF16))
    o_all = jnp.concatenate(heads, axis=-1)
    x2 = x1 + jnp.dot(o_all, wo_ref[...], preferred_element_type=F32)
    x2_ref[0] = x2
    h3 = _rms(x2, gm_ref[...])

    h_hi = h3.astype(BF16)
    h_lo = (h3 - h_hi.astype(F32)).astype(BF16)
    h3p_ref[0] = _pack_bf16_pairs(h_hi.astype(F32))
    nt = (((1,), (1,)), ((), ()))
    logits = (lax.dot_general(h_hi, wr_hi_ref[...], nt, preferred_element_type=F32)
              + lax.dot_general(h_lo, wr_hi_ref[...], nt, preferred_element_type=F32)
              + lax.dot_general(h_hi, wr_lo_ref[...], nt, preferred_element_type=F32)) + br_ref[...]

    lane = lax.broadcasted_iota(jnp.int32, (tm, LANES), 1)
    lane_f = lane.astype(F32)
    big = float(LANES)
    cmask = lane < N_EXPERT_GROUPS
    lc = jnp.where(cmask, logits, NEG)
    cmax = jnp.max(lc, axis=-1, keepdims=True)
    g_idx = jnp.min(jnp.where(lc == cmax, lane_f, big), axis=-1, keepdims=True)
    g_gate = 1.0 / jnp.sum(jnp.exp(lc - cmax), axis=-1, keepdims=True)
    grp_f = ((lane - N_EXPERT_GROUPS) >> 3).astype(F32)
    in_fine = jnp.where(lane >= N_EXPERT_GROUPS, jnp.where(lane < N_EXPERT_GROUPS + N_EXPERTS, grp_f, -1.0), -1.0)
    lf = jnp.where(in_fine == g_idx, logits, NEG)
    v1 = jnp.max(lf, axis=-1, keepdims=True)
    i1 = jnp.min(jnp.where(lf == v1, lane_f, big), axis=-1, keepdims=True)
    lf2 = jnp.where(lane_f == i1, NEG, lf)
    v2 = jnp.max(lf2, axis=-1, keepdims=True)
    i2 = jnp.min(jnp.where(lf2 == v2, lane_f, big), axis=-1, keepdims=True)
    e21 = jnp.exp(v2 - v1)
    den = 1.0 / (1.0 + e21)
    two = lax.broadcasted_iota(jnp.int32, (tm, 2), 1)
    eid_ref[0] = jnp.where(two == 0, i1, i2).astype(jnp.int32) - N_EXPERT_GROUPS
    gate_ref[0] = jnp.where(two == 0, g_gate * den, g_gate * (e21 * den))


def _post_mix(x3d, a_n, p_n, w_out_bf, g_cross, w_q_bf, k_mem, v_mem, w_o_bf, g_moe, wr_hi, wr_lo, b_r, tm=256):
    b, seq_len, _ = x3d.shape
    tok = lambda w: pl.BlockSpec((1, tm, w), lambda bi, i: (bi, i, 0))
    mem = pl.BlockSpec((1, N_MEM, MEM_WIDTH), lambda bi, i: (bi, 0, 0))
    return pl.pallas_call(
        _post_kernel,
        grid=(b, seq_len // tm),
        in_specs=[tok(D_MODEL), tok(NA_WIDTH), tok(POOL_WIDTH), _const_spec(w_out_bf.shape),
                  _const_spec((1, D_MODEL)), _const_spec(w_q_bf.shape), mem, mem, _const_spec(w_o_bf.shape),
                  _const_spec((1, D_MODEL)), _const_spec(wr_hi.shape), _const_spec(wr_lo.shape),
                  _const_spec((1, LANES))],
        out_specs=[tok(D_MODEL), tok(D_MODEL // 2), tok(2), tok(2)],
        out_shape=[jax.ShapeDtypeStruct((b, seq_len, D_MODEL), F32),
                   jax.ShapeDtypeStruct((b, seq_len, D_MODEL // 2), jnp.uint32),
                   jax.ShapeDtypeStruct((b, seq_len, 2), jnp.int32), jax.ShapeDtypeStruct((b, seq_len, 2), F32)],
        compiler_params=_params("parallel", "parallel"),
        name="post_mix",
    )(x3d, a_n, p_n, w_out_bf, g_cross, w_q_bf, k_mem, v_mem, w_o_bf, g_moe, wr_hi, wr_lo, b_r)


def _rank_kernel(eid_ref, rank_ref, cnt_ref, carry):
    tm = eid_ref.shape[0]

    @pl.when(pl.program_id(0) == 0)
    def _():
        carry[...] = jnp.zeros_like(carry)

    lane = lax.broadcasted_iota(jnp.int32, (tm, LANES), 1)
    e1 = eid_ref[:, 0:1]
    e2 = eid_ref[:, 1:2]
    hit1 = lane == e1
    hit2 = lane == e2
    onehot = jnp.where(hit1, 1.0, 0.0) + jnp.where(hit2, 1.0, 0.0)
    row = lax.broadcasted_iota(jnp.int32, (tm, tm), 0)
    col = lax.broadcasted_iota(jnp.int32, (tm, tm), 1)
    earlier = jnp.where(col < row, 1.0, 0.0).astype(BF16)
    before = jnp.dot(earlier, onehot.astype(BF16), preferred_element_type=F32) + carry[...]
    r1 = jnp.sum(jnp.where(hit1, before, 0.0), axis=-1, keepdims=True)
    r2 = jnp.sum(jnp.where(hit2, before, 0.0), axis=-1, keepdims=True)
    two = lax.broadcasted_iota(jnp.int32, (tm, 2), 1)
    rank_ref[...] = jnp.where(two == 0, r1, r2).astype(jnp.int32)
    total = carry[...] + jnp.sum(onehot, axis=0, keepdims=True)
    carry[...] = total
    cnt_ref[...] = total.astype(jnp.int32)


def _expert_ranks(eid2d, tm=512):
    t = eid2d.shape[0]
    return pl.pallas_call(
        _rank_kernel,
        grid=(t // tm,),
        in_specs=[pl.BlockSpec((tm, 2), lambda i: (i, 0))],
        out_specs=[pl.BlockSpec((tm, 2), lambda i: (i, 0)), pl.BlockSpec((1, LANES), lambda i: (0, 0))],
        out_shape=[jax.ShapeDtypeStruct((t, 2), jnp.int32), jax.ShapeDtypeStruct((1, LANES), jnp.int32)],
        scratch_shapes=[pltpu.VMEM((1, LANES), F32)],
        compiler_params=_params("arbitrary"),
        name="expert_ranks",
    )(eid2d)


def _dispatch_kernel(dest_ref, pad_lo_ref, pad_hi_ref, h_ref, xs_ref, zrow, sem, zsem, *, tm, n_tok):
    i = pl.program_id(0)
    base = i * tm
    unroll = 8

    def row_copy(r, k):
        return pltpu.make_async_copy(h_ref.at[pl.ds(r, 1)],
                                     xs_ref.at[pl.ds(dest_ref[k * n_tok + base + r], 1)], sem)

    def issue(c, carry):
        for u in range(unroll):
            for k in range(2):
                row_copy(c * unroll + u, k).start()
        return carry

    lax.fori_loop(0, tm // unroll, issue, 0)

    @pl.when(i == 0)
    def _():
        zrow[...] = jnp.zeros_like(zrow)

        def zero_copy(s):
            return pltpu.make_async_copy(zrow, xs_ref.at[pl.ds(s, 1)], zsem)

        def per_range(e, carry):
            def start(s, c2):
                zero_copy(s).start()
                return c2

            def wait(s, c2):
                zero_copy(s).wait()
                return c2

            lax.fori_loop(pad_lo_ref[e], pad_hi_ref[e], start, 0)
            lax.fori_loop(pad_lo_ref[e], pad_hi_ref[e], wait, 0)
            return carry

        lax.fori_loop(0, pad_lo_ref.shape[0], per_range, 0)

    def drain(c, carry):
        for u in range(unroll):
            for k in range(2):
                row_copy(c * unroll + u, k).wait()
        return carry

    lax.fori_loop(0, tm // unroll, drain, 0)


def _dispatch(dest_kmajor, pad_lo, pad_hi, h3p_2d, n_slots, tm=512):
    t, half = h3p_2d.shape
    kern = functools.partial(_dispatch_kernel, tm=tm, n_tok=t)
    return pl.pallas_call(
        kern,
        grid_spec=pltpu.PrefetchScalarGridSpec(
            num_scalar_prefetch=3,
            grid=(t // tm,),
            in_specs=[pl.BlockSpec((tm, half), lambda i, *_: (i, 0))],
            out_specs=pl.BlockSpec(memory_space=pl.ANY),
            scratch_shapes=[pltpu.VMEM((1, half), jnp.uint32), pltpu.SemaphoreType.DMA, pltpu.SemaphoreType.DMA],
        ),
        out_shape=jax.ShapeDtypeStruct((n_slots, half), jnp.uint32),
        compiler_params=pltpu.CompilerParams(dimension_semantics=("arbitrary",), has_side_effects=True),
        name="moe_dispatch",
    )(dest_kmajor, pad_lo, pad_hi, h3p_2d)


def _row_gather(src_ref, idx_ref, idx_base, n_rows, dst_ref, sem):
    for r in range(n_rows):
        pltpu.make_async_copy(src_ref.at[pl.ds(idx_ref[idx_base + r], 1)], dst_ref.at[pl.ds(r, 1)], sem).start()


def _row_gather_wait(src_ref, n_rows, dst_ref, sem):
    for r in range(n_rows):
        pltpu.make_async_copy(src_ref.at[pl.ds(0, 1)], dst_ref.at[pl.ds(r, 1)], sem).wait()


def _expert_kernel(be_ref, nused_ref, xs_ref, wgu_ref, wd_ref, ys_ref):
    del be_ref
    i = pl.program_id(0)

    @pl.when(i < nused_ref[0])
    def _():
        x_lo, x_hi = _unpack_bf16_pairs(xs_ref[...])
        half = x_lo.shape[1]
        gu = (jnp.dot(x_lo.astype(BF16), wgu_ref[0, :half, :], preferred_element_type=F32)
              + jnp.dot(x_hi.astype(BF16), wgu_ref[0, half:, :], preferred_element_type=F32))
        gate = gu[:, :D_EXPERT]
        hid = gate * jax.nn.sigmoid(gate) * gu[:, D_EXPERT:]
        y = jnp.dot(hid.astype(BF16), wd_ref[0], preferred_element_type=F32)
        ys_ref[...] = _pack_bf16_pairs(y.astype(BF16).astype(F32))

    @pl.when(i >= nused_ref[0])
    def _():
        ys_ref[...] = jnp.zeros_like(ys_ref)


def _experts(block_expert, n_used, xs, w_gu_bf, w_d_bf):
    n_slots, half = xs.shape
    d = 2 * half
    return pl.pallas_call(
        _expert_kernel,
        grid_spec=pltpu.PrefetchScalarGridSpec(
            num_scalar_prefetch=2,
            grid=(n_slots // MOE_BLOCK,),
            in_specs=[pl.BlockSpec((MOE_BLOCK, half), lambda i, be, nu: (i, 0)),
                      pl.BlockSpec((1, d, 2 * D_EXPERT), lambda i, be, nu: (be[i], 0, 0)),
                      pl.BlockSpec((1, D_EXPERT, d), lambda i, be, nu: (be[i], 0, 0))],
            out_specs=pl.BlockSpec((MOE_BLOCK, half), lambda i, be, nu: (i, 0)),
        ),
        out_shape=jax.ShapeDtypeStruct((n_slots, half), jnp.uint32),
        compiler_params=_params("arbitrary"),
        name="moe_experts",
    )(block_expert, n_used, xs, w_gu_bf, w_d_bf)


def _combine_kernel(dest_ref, x2_ref, gate_ref, g_ref, ys_ref, o_ref, ybuf, sems, *, tm):
    i = pl.program_id(0)
    n = pl.num_programs(0)
    slot = i % 2
    unroll = 8

    n_tok = n * tm

    def start_tile(tile, buf_slot):
        def issue(c, carry):
            for k in range(2):
                _row_gather(ys_ref, dest_ref, k * n_tok + tile * tm + c * unroll, unroll,
                            ybuf.at[buf_slot, k, pl.ds(c * unroll, unroll)], sems.at[buf_slot])
            return carry

        lax.fori_loop(0, tm // unroll, issue, 0)

    @pl.when(i == 0)
    def _():
        start_tile(0, 0)

    @pl.when(i + 1 < n)
    def _():
        start_tile(i + 1, 1 - slot)

    def drain(c, carry):
        for k in range(2):
            _row_gather_wait(ys_ref, unroll, ybuf.at[slot, k, pl.ds(c * unroll, unroll)], sems.at[slot])
        return carry

    lax.fori_loop(0, tm // unroll, drain, 0)
    gates = gate_ref[...]
    lo0, hi0 = _unpack_bf16_pairs(ybuf[slot, 0])
    lo1, hi1 = _unpack_bf16_pairs(ybuf[slot, 1])
    y = jnp.concatenate([gates[:, 0:1] * lo0 + gates[:, 1:2] * lo1,
                         gates[:, 0:1] * hi0 + gates[:, 1:2] * hi1], axis=-1)
    o_ref[...] = _rms(x2_ref[...] + y, g_ref[...])


def _combine(dest_flat, x2_2d, gate2d, g_final, ys, tm=256):
    t, d = x2_2d.shape
    kern = functools.partial(_combine_kernel, tm=tm)
    return pl.pallas_call(
        kern,
        grid_spec=pltpu.PrefetchScalarGridSpec(
            num_scalar_prefetch=1,
            grid=(t // tm,),
            in_specs=[pl.BlockSpec((tm, d), lambda i, dst: (i, 0)),
                      pl.BlockSpec((tm, 2), lambda i, dst: (i, 0)),
                      pl.BlockSpec((1, d), lambda i, dst: (0, 0)),
                      pl.BlockSpec(memory_space=pl.ANY)],
            out_specs=pl.BlockSpec((tm, d), lambda i, dst: (i, 0)),
            scratch_shapes=[pltpu.VMEM((2, 2, tm, d // 2), jnp.uint32), pltpu.SemaphoreType.DMA((2,))],
        ),
        out_shape=jax.ShapeDtypeStruct((t, d), F32),
        compiler_params=_params("arbitrary"),
        name="moe_combine",
    )(dest_flat, x2_2d, gate2d, g_final, ys)


def _encode(x, mem, p):
    b, seq_len, d = x.shape
    t = b * seq_len
    q, k, v, u = _in_proj(x.reshape(t, d), p["g_mix"], p["w_in"])
    shape3 = (b, seq_len, NA_WIDTH)
    a_n = _neighborhood_attention(q.reshape(shape3), k.reshape(shape3), v.reshape(shape3), p["na_bias"], p["g_branch_a"])
    p_n = _pool_mixer(u.reshape(shape3), p["w_pool"], p["pool_scale"], p["g_branch_b"])
    kv = _norm_matmul(mem.reshape(b * N_MEM, d), p["g_mem"], p["w_kv"], BF16).reshape(b, N_MEM, 2 * MEM_WIDTH)
    x2, h3p, eid, gate = _post_mix(x, a_n, p_n, p["w_out"], p["g_cross"], p["w_q"], kv[:, :, :MEM_WIDTH],
                                   kv[:, :, MEM_WIDTH:], p["w_o"], p["g_moe"], p["wr_hi"], p["wr_lo"], p["b_r"])
    x2 = x2.reshape(t, d)
    eid2d = eid.reshape(t, 2)
    rank, counts = _expert_ranks(eid2d)
    counts = counts[0, :N_EXPERTS]
    padded = ((counts + MOE_BLOCK - 1) // MOE_BLOCK) * MOE_BLOCK
    padded_ends = jnp.cumsum(padded)
    padded_starts = padded_ends - padded
    n_blocks = -(-(2 * t + N_EXPERTS * (MOE_BLOCK - 1)) // MOE_BLOCK)
    n_slots = n_blocks * MOE_BLOCK
    block_pos = jnp.arange(n_blocks, dtype=jnp.int32) * MOE_BLOCK
    block_expert = jnp.minimum(jnp.sum(padded_ends[None, :] <= block_pos[:, None], axis=1), N_EXPERTS - 1).astype(jnp.int32)
    is_e = eid2d[:, :, None] == jnp.arange(N_EXPERTS, dtype=jnp.int32)
    dest = (jnp.sum(jnp.where(is_e, padded_starts, 0), axis=-1) + rank).astype(jnp.int32)
    dest_kmajor = dest.T.reshape(-1)
    pad_lo = jnp.concatenate([padded_starts + counts, padded_ends[-1:]]).astype(jnp.int32)
    pad_hi = jnp.concatenate([padded_ends, jnp.full((1,), n_slots, jnp.int32)]).astype(jnp.int32)
    n_used = (padded_ends[-1:] // MOE_BLOCK).astype(jnp.int32)
    xs = _dispatch(dest_kmajor, pad_lo, pad_hi, h3p.reshape(t, d // 2), n_slots)
    ys = _experts(block_expert, n_used, xs, p["w_gu"], p["w_d"])
    y = _combine(dest_kmajor, x2, gate.reshape(t, 2), p["g_final"], ys)
    return y.reshape(b, seq_len, d)


def kernel(x_prompt, x_sample, mem_prompt, mem_sample, g_mix, w_in, rpb, w_pool, pool_scale, g_branch_a, g_branch_b, w_out, g_cross, g_mem, w_q_mem, w_kv_mem, w_o_mem, g_moe, w_coarse, b_coarse, w_fine, b_fine, w_gate_e, w_up_e, w_down_e, g_final):
    depth = w_in.shape[0]
    xp, xs = x_prompt, x_sample
    row = lambda a: a.reshape(1, -1).astype(F32)
    for l in range(depth):
        w_r = jnp.concatenate([w_coarse[l].T, w_fine[l].T], axis=0).astype(F32)
        w_r = jnp.pad(w_r, ((0, LANES - w_r.shape[0]), (0, 0)))
        wr_hi = w_r.astype(BF16)
        b_r = jnp.concatenate([b_coarse[l], b_fine[l]]).astype(F32)
        p = {
            "g_mix": row(g_mix[l]), "w_in": w_in[l].astype(BF16),
            "na_bias": _na_bias_table(rpb[l]), "g_branch_a": row(g_branch_a[l]),
            "w_pool": w_pool[l].astype(BF16), "pool_scale": row(pool_scale[l]), "g_branch_b": row(g_branch_b[l]),
            "w_out": w_out[l].astype(BF16), "g_cross": row(g_cross[l]), "g_mem": row(g_mem[l]),
            "w_q": w_q_mem[l].astype(BF16), "w_kv": w_kv_mem[l].astype(BF16), "w_o": w_o_mem[l].astype(BF16),
            "g_moe": row(g_moe[l]),
            "wr_hi": wr_hi, "wr_lo": (w_r - wr_hi.astype(F32)).astype(BF16),
            "b_r": jnp.pad(b_r, (0, LANES - b_r.shape[0])).reshape(1, LANES),
            "w_gu": jnp.concatenate([w_gate_e[l], w_up_e[l]], axis=-1).astype(BF16),
            "w_d": w_down_e[l].astype(BF16),
            "g_final": row(g_final),
        }
        assert depth == 1, "final-norm fusion assumes a single layer"
        xp = _encode(xp, mem_prompt, p)
        xs = _encode(xs, mem_sample, p)
    return (xp, xs)
```

```python
import functools

import jax
import jax.numpy as jnp
from jax import lax
from jax.experimental import pallas as pl
from jax.experimental.pallas import tpu as pltpu

F32 = jnp.float32
BF16 = jnp.bfloat16

D_MODEL = 2048
GRID_W = 64
NA_HEADS = 16
NA_WIDTH = 1024
NA_HEAD_DIM = 64
WIN_H = 8
WIN_W = 16
POOL_WINDOWS = (2, 4, 8, 16)
POOL_WIDTH = 1024
POOL_GROUP_DIM = 256
POOL_HALO = 16
N_MEM = 256
MEM_HEADS = 4
MEM_HEAD_DIM = 128
MEM_WIDTH = 512
N_EXPERT_GROUPS = 4
EXPERTS_PER_GROUP = 8
N_EXPERTS = 32
D_EXPERT = 512
MOE_BLOCK = 256
EPS = 1e-6
NEG = -1e30
LANES = 128
ROW_GROUP = 8
KV_CHUNK_ROWS = 4
SOFTMAX_ROWS = 16
VMEM_LIMIT = 56 * 1024 * 1024


def _rms(x, g):
    return x * lax.rsqrt(jnp.mean(x * x, axis=-1, keepdims=True) + EPS) * g


def _pack_bf16_pairs(x):
    n = x.shape[1] // 2
    bits = lax.bitcast_convert_type(x, jnp.uint32)
    return (bits[:, :n] >> 16) | (bits[:, n:] & jnp.uint32(0xFFFF0000))


def _unpack_bf16_pairs(u):
    lo = lax.bitcast_convert_type(u << 16, F32)
    hi = lax.bitcast_convert_type(u & jnp.uint32(0xFFFF0000), F32)
    return lo, hi


def _const_spec(shape):
    nd = len(shape)
    return pl.BlockSpec(shape, lambda *_: (0,) * nd, pipeline_mode=pl.Buffered(1))


def _params(*sem):
    return pltpu.CompilerParams(dimension_semantics=sem, vmem_limit_bytes=VMEM_LIMIT)


def _in_proj_kernel(x_ref, g_ref, w_ref, q_ref, k_ref, v_ref, u_ref):
    h = _rms(x_ref[...], g_ref[...]).astype(BF16)
    q_ref[...] = (jnp.dot(h, w_ref[:, 0:NA_WIDTH], preferred_element_type=F32)
                  * (NA_HEAD_DIM ** -0.5)).astype(BF16)
    k_ref[...] = jnp.dot(h, w_ref[:, NA_WIDTH:2 * NA_WIDTH], preferred_element_type=F32).astype(BF16)
    v_ref[...] = jnp.dot(h, w_ref[:, 2 * NA_WIDTH:3 * NA_WIDTH], preferred_element_type=F32).astype(BF16)
    u_ref[...] = jnp.dot(h, w_ref[:, 3 * NA_WIDTH:], preferred_element_type=F32)


def _in_proj(x2d, g_mix, w_in_bf, tm=512):
    t = x2d.shape[0]
    row = lambda i: (i, 0)
    return pl.pallas_call(
        _in_proj_kernel,
        grid=(t // tm,),
        in_specs=[pl.BlockSpec((tm, D_MODEL), row), _const_spec((1, D_MODEL)),
                  _const_spec(w_in_bf.shape)],
        out_specs=[pl.BlockSpec((tm, NA_WIDTH), row)] * 4,
        out_shape=[jax.ShapeDtypeStruct((t, NA_WIDTH), BF16)] * 3
        + [jax.ShapeDtypeStruct((t, POOL_WIDTH), F32)],
        compiler_params=_params("parallel"),
        name="in_proj",
    )(x2d, g_mix, w_in_bf)


def _norm_matmul_kernel(x_ref, g_ref, w_ref, o_ref):
    h = _rms(x_ref[...], g_ref[...]).astype(BF16)
    o_ref[...] = jnp.dot(h, w_ref[...], preferred_element_type=F32).astype(o_ref.dtype)


def _norm_matmul(x2d, g, w_bf, out_dtype, tm=256):
    t, k = x2d.shape
    n = w_bf.shape[1]
    return pl.pallas_call(
        _norm_matmul_kernel,
        grid=(t // tm,),
        in_specs=[pl.BlockSpec((tm, k), lambda i: (i, 0)), _const_spec((1, k)), _const_spec((k, n))],
        out_specs=pl.BlockSpec((tm, n), lambda i: (i, 0)),
        out_shape=jax.ShapeDtypeStruct((t, n), out_dtype),
        compiler_params=_params("parallel"),
        name="norm_matmul",
    )(x2d, g, w_bf)


def _pool_kernel(prev_ref, cur_ref, next_ref, w_ref, scale_ref, g_ref, o_ref, ext_ref, mix_ref, *, seq_len, tm):
    i = pl.program_id(1)
    n_i = pl.num_programs(1)
    ext_ref[0:POOL_HALO, :] = jnp.where(i > 0, prev_ref[0], 0.0)
    ext_ref[POOL_HALO:POOL_HALO + tm, :] = cur_ref[0]
    ext_ref[POOL_HALO + tm:, :] = jnp.where(i < n_i - 1, next_ref[0], 0.0)
    n_ext = tm + 2 * POOL_HALO
    pos = i * tm + lax.broadcasted_iota(jnp.int32, (tm, 1), 0)
    for gi, w in enumerate(POOL_WINDOWS):
        cols = slice(gi * POOL_GROUP_DIM, (gi + 1) * POOL_GROUP_DIM)
        e = ext_ref[:, cols]
        s = e + pltpu.roll(e, 1, 0)
        half = 1
        while 2 * half < w:
            s = pltpu.roll(s, half, 0) + pltpu.roll(s, n_ext - half, 0)
            half *= 2
        lo = jnp.maximum(pos - w // 2, 0)
        hi = jnp.minimum(pos - w // 2 + w, seq_len)
        cnt = (hi - lo).astype(F32)
        pooled = s[POOL_HALO:POOL_HALO + tm] / cnt - e[POOL_HALO:POOL_HALO + tm]
        mix_ref[:, cols] = jnp.dot(pooled.astype(BF16), w_ref[gi], preferred_element_type=F32)
    mixed = mix_ref[...] * scale_ref[...]
    o_ref[0] = _rms(mixed, g_ref[...]).astype(BF16)


def _pool_mixer(u3d, w_pool_bf, pool_scale, g_b, tm=512):
    b, seq_len, _ = u3d.shape
    tm = min(tm, seq_len)
    hb = tm // POOL_HALO
    n_hb = seq_len // POOL_HALO
    kern = functools.partial(_pool_kernel, seq_len=seq_len, tm=tm)
    return pl.pallas_call(
        kern,
        grid=(b, seq_len // tm),
        in_specs=[
            pl.BlockSpec((1, POOL_HALO, POOL_WIDTH), lambda bi, i: (bi, jnp.maximum(i * hb - 1, 0), 0)),
            pl.BlockSpec((1, tm, POOL_WIDTH), lambda bi, i: (bi, i, 0)),
            pl.BlockSpec((1, POOL_HALO, POOL_WIDTH), lambda bi, i: (bi, jnp.minimum((i + 1) * hb, n_hb - 1), 0)),
            _const_spec(w_pool_bf.shape), _const_spec((1, POOL_WIDTH)), _const_spec((1, POOL_WIDTH)),
        ],
        out_specs=pl.BlockSpec((1, tm, POOL_WIDTH), lambda bi, i: (bi, i, 0)),
        out_shape=jax.ShapeDtypeStruct((b, seq_len, POOL_WIDTH), BF16),
        scratch_shapes=[pltpu.VMEM((tm + 2 * POOL_HALO, POOL_WIDTH), F32), pltpu.VMEM((tm, POOL_WIDTH), F32)],
        compiler_params=_params("parallel", "parallel"),
        name="pool_mixer",
    )(u3d, u3d, u3d, w_pool_bf, pool_scale, g_b)


def _na_bias_table(rpb):
    c = jnp.arange(GRID_W)[:, None]
    kc = jnp.arange(GRID_W)[None, :]
    start = jnp.clip(c - WIN_W // 2, 0, GRID_W - WIN_W)
    valid = (kc >= start) & (kc < start + WIN_W)
    dc = kc - c + (WIN_W - 1)
    onehot = (dc[:, :, None] == jnp.arange(2 * WIN_W - 1)[None, None, :]).astype(F32)
    band = jnp.einsum("hrd,cqd->hrcq", rpb.astype(F32), onehot, precision=lax.Precision.HIGHEST)
    band = jnp.where(valid[None, None], band, NEG)
    n_dr = 2 * WIN_H - 2
    units = jnp.concatenate([band[:, :n_dr], band[:, 1:n_dr + 1]], axis=-1)
    units = units.reshape(NA_HEADS // 2, 2, n_dr, GRID_W, 2 * GRID_W)
    return jnp.transpose(units, (0, 2, 1, 3, 4)).reshape(NA_HEADS // 2, n_dr, 2 * GRID_W, 2 * GRID_W)


def _na_kernel(q_ref, k0, k1, k2, k3, v0, v1, v2, v3, bias_ref, g_ref, o_ref, kbuf, vbuf, acc, s_scr, p_scr, linv_scr,
               *, n_rows):
    g = pl.program_id(1)
    chunk = KV_CHUNK_ROWS * GRID_W
    for c, (kr, vr) in enumerate(((k0, v0), (k1, v1), (k2, v2), (k3, v3))):
        kbuf[c * chunk:(c + 1) * chunk, :] = kr[0]
        vbuf[c * chunk:(c + 1) * chunk, :] = vr[0]
    first_head = lax.broadcasted_iota(jnp.int32, (GRID_W, LANES), 1) < NA_HEAD_DIM
    n_keys = WIN_H * GRID_W

    n_pairs = NA_HEADS // 2

    def row_params(j):
        r = g * ROW_GROUP + j
        r0 = jnp.clip(r - WIN_H // 2, 0, n_rows - WIN_H)
        pat = r - r0
        koff = pl.multiple_of((r0 - (g * ROW_GROUP - KV_CHUNK_ROWS)) * GRID_W, GRID_W)
        qoff = pl.multiple_of(j * GRID_W, GRID_W)
        return pat, koff, qoff

    def scores(slot, prm):
        _, koff, qoff = prm
        for hp in range(n_pairs):
            cols = slice(hp * LANES, (hp + 1) * LANES)
            q2 = q_ref[0, pl.ds(qoff, GRID_W), cols]
            zero = jnp.zeros_like(q2)
            qq = jnp.concatenate([jnp.where(first_head, q2, zero), jnp.where(first_head, zero, q2)], axis=0)
            kk = kbuf[pl.ds(koff, n_keys), cols]
            s_scr[slot, hp] = lax.dot_general(qq, kk, (((1,), (1,)), ((), ())), preferred_element_type=F32)

    def softmax(slot, prm):
        pat = prm[0]
        for hp in range(n_pairs):
            for rc in range(2 * GRID_W // SOFTMAX_ROWS):
                rows = slice(rc * SOFTMAX_ROWS, (rc + 1) * SOFTMAX_ROWS)
                bias = jnp.concatenate([bias_ref[hp, 2 * u - pat + (WIN_H - 1), rows, :] for u in range(WIN_H // 2)],
                                       axis=-1)
                s = s_scr[slot, hp, rows, :] + bias
                p = jnp.exp(s - jnp.max(s, axis=-1, keepdims=True))
                p_scr[slot, hp, rows, :] = p.astype(BF16)
                linv = 1.0 / jnp.sum(p, axis=-1, keepdims=True)
                linv_scr[slot, hp, rows, :] = jnp.broadcast_to(linv, (SOFTMAX_ROWS, LANES))

    def outputs(slot, prm):
        _, koff, qoff = prm
        for hp in range(n_pairs):
            cols = slice(hp * LANES, (hp + 1) * LANES)
            vv = vbuf[pl.ds(koff, n_keys), cols]
            o = jnp.dot(p_scr[slot, hp], vv, preferred_element_type=F32) * linv_scr[slot, hp]
            acc[pl.ds(qoff, GRID_W), cols] = jnp.where(first_head, o[:GRID_W], o[GRID_W:])

    def row_pair_body(jj, carry):
        prms = [row_params(2 * jj + s) for s in range(2)]
        for phase in (scores, softmax, outputs):
            for s in range(2):
                phase(s, prms[s])
        return carry

    lax.fori_loop(0, ROW_GROUP // 2, row_pair_body, 0)
    o_ref[0] = _rms(acc[...], g_ref[...]).astype(BF16)


def _neighborhood_attention(q3d, k3d, v3d, bias_tbl, g_a):
    b, seq_len, _ = q3d.shape
    n_rows = seq_len // GRID_W
    tq = ROW_GROUP * GRID_W
    chunk = KV_CHUNK_ROWS * GRID_W
    n_chunks = seq_len // chunk

    def kv_spec(c):
        return pl.BlockSpec((1, chunk, NA_WIDTH),
                            lambda bi, g: (bi, jnp.clip(2 * g - 1 + c, 0, n_chunks - 1), 0))

    kern = functools.partial(_na_kernel, n_rows=n_rows)
    return pl.pallas_call(
        kern,
        grid=(b, n_rows // ROW_GROUP),
        in_specs=[pl.BlockSpec((1, tq, NA_WIDTH), lambda bi, g: (bi, g, 0))]
        + [kv_spec(c) for c in range(4)] * 2
        + [_const_spec(bias_tbl.shape), _const_spec((1, NA_WIDTH))],
        out_specs=pl.BlockSpec((1, tq, NA_WIDTH), lambda bi, g: (bi, g, 0)),
        out_shape=jax.ShapeDtypeStruct((b, seq_len, NA_WIDTH), BF16),
        scratch_shapes=[pltpu.VMEM((4 * chunk, NA_WIDTH), BF16), pltpu.VMEM((4 * chunk, NA_WIDTH), BF16),
                        pltpu.VMEM((tq, NA_WIDTH), F32),
                        pltpu.VMEM((2, NA_HEADS // 2, 2 * GRID_W, WIN_H * GRID_W), F32),
                        pltpu.VMEM((2, NA_HEADS // 2, 2 * GRID_W, WIN_H * GRID_W), BF16),
                        pltpu.VMEM((2, NA_HEADS // 2, 2 * GRID_W, LANES), F32)],
        compiler_params=_params("parallel", "parallel"),
        name="neighborhood_attention",
    )(q3d, k3d, k3d, k3d, k3d, v3d, v3d, v3d, v3d, bias_tbl, g_a)


def _post_kernel(x_ref, a_ref, p_ref, wout_ref, gc_ref, wq_ref, km_ref, vm_ref, wo_ref, gm_ref,
                 wr_ref, br_ref, x2_ref, h3p_ref, eid_ref, gate_ref):
    tm = x_ref.shape[1]
    x1 = (x_ref[0]
          + jnp.dot(a_ref[0], wout_ref[0:NA_WIDTH, :], preferred_element_type=F32)
          + jnp.dot(p_ref[0], wout_ref[NA_WIDTH:, :], preferred_element_type=F32))
    h2 = _rms(x1, gc_ref[...]).astype(BF16)
    qm = jnp.dot(h2, wq_ref[...], preferred_element_type=F32).astype(BF16)
    heads = []
    for hd in range(MEM_HEADS):
        cols = slice(hd * MEM_HEAD_DIM, (hd + 1) * MEM_HEAD_DIM)
        s = lax.dot_general(qm[:, cols], km_ref[0, :, cols], (((1,), (1,)), ((), ())),
                            preferred_element_type=F32) * (MEM_HEAD_DIM ** -0.5)
        m = jnp.max(s, axis=-1, keepdims=True)
        p = jnp.exp(s - m)
        l = jnp.sum(p, axis=-1, keepdims=True)
        o = jnp.dot(p.astype(BF16), vm_ref[0, :, cols], preferred_element_type=F32) / l
        heads.append(o.astype(BF16))
    o_all = jnp.concatenate(heads, axis=-1)
    x2 = x1 + jnp.dot(o_all, wo_ref[...], preferred_element_type=F32)
    x2_ref[0] = x2
    h3 = _rms(x2, gm_ref[...])

    h_bf = h3.astype(BF16)
    h3p_ref[0] = _pack_bf16_pairs(h_bf.astype(F32))
    logits = lax.dot_general(h_bf, wr_ref[...], (((1,), (1,)), ((), ())), preferred_element_type=F32) + br_ref[...]

    lane = lax.broadcasted_iota(jnp.int32, (tm, LANES), 1)
    lane_f = lane.astype(F32)
    big = float(LANES)
    cmask = lane < N_EXPERT_GROUPS
    lc = jnp.where(cmask, logits, NEG)
    cmax = jnp.max(lc, axis=-1, keepdims=True)
    g_idx = jnp.min(jnp.where(lc == cmax, lane_f, big), axis=-1, keepdims=True)
    g_gate = 1.0 / jnp.sum(jnp.exp(lc - cmax), axis=-1, keepdims=True)
    grp_f = ((lane - N_EXPERT_GROUPS) >> 3).astype(F32)
    in_fine = jnp.where(lane >= N_EXPERT_GROUPS, jnp.where(lane < N_EXPERT_GROUPS + N_EXPERTS, grp_f, -1.0), -1.0)
    lf = jnp.where(in_fine == g_idx, logits, NEG)
    v1 = jnp.max(lf, axis=-1, keepdims=True)
    i1 = jnp.min(jnp.where(lf == v1, lane_f, big), axis=-1, keepdims=True)
    lf2 = jnp.where(lane_f == i1, NEG, lf)
    v2 = jnp.max(lf2, axis=-1, keepdims=True)
    i2 = jnp.min(jnp.where(lf2 == v2, lane_f, big), axis=-1, keepdims=True)
    e21 = jnp.exp(v2 - v1)
    den = 1.0 / (1.0 + e21)
    two = lax.broadcasted_iota(jnp.int32, (tm, 2), 1)
    eid_ref[0] = jnp.where(two == 0, i1, i2).astype(jnp.int32) - N_EXPERT_GROUPS
    gate_ref[0] = jnp.where(two == 0, g_gate * den, g_gate * (e21 * den))


def _post_mix(x3d, a_n, p_n, w_out_bf, g_cross, w_q_bf, k_mem, v_mem, w_o_bf, g_moe, w_r_bf, b_r, tm=512):
    b, seq_len, _ = x3d.shape
    tok = lambda w: pl.BlockSpec((1, tm, w), lambda bi, i: (bi, i, 0))
    mem = pl.BlockSpec((1, N_MEM, MEM_WIDTH), lambda bi, i: (bi, 0, 0))
    return pl.pallas_call(
        _post_kernel,
        grid=(b, seq_len // tm),
        in_specs=[tok(D_MODEL), tok(NA_WIDTH), tok(POOL_WIDTH), _const_spec(w_out_bf.shape),
                  _const_spec((1, D_MODEL)), _const_spec(w_q_bf.shape), mem, mem, _const_spec(w_o_bf.shape),
                  _const_spec((1, D_MODEL)), _const_spec(w_r_bf.shape),
                  _const_spec((1, LANES))],
        out_specs=[tok(D_MODEL), tok(D_MODEL // 2), tok(2), tok(2)],
        out_shape=[jax.ShapeDtypeStruct((b, seq_len, D_MODEL), F32),
                   jax.ShapeDtypeStruct((b, seq_len, D_MODEL // 2), jnp.uint32),
                   jax.ShapeDtypeStruct((b, seq_len, 2), jnp.int32), jax.ShapeDtypeStruct((b, seq_len, 2), F32)],
        compiler_params=_params("parallel", "parallel"),
        name="post_mix",
    )(x3d, a_n, p_n, w_out_bf, g_cross, w_q_bf, k_mem, v_mem, w_o_bf, g_moe, w_r_bf, b_r)


def _rank_kernel(eid_ref, rank_ref, cnt_ref, carry):
    tm = eid_ref.shape[0]

    @pl.when(pl.program_id(0) == 0)
    def _():
        carry[...] = jnp.zeros_like(carry)

    lane = lax.broadcasted_iota(jnp.int32, (tm, LANES), 1)
    e1 = eid_ref[:, 0:1]
    e2 = eid_ref[:, 1:2]
    hit1 = lane == e1
    hit2 = lane == e2
    onehot = jnp.where(hit1, 1.0, 0.0) + jnp.where(hit2, 1.0, 0.0)
    row = lax.broadcasted_iota(jnp.int32, (tm, tm), 0)
    col = lax.broadcasted_iota(jnp.int32, (tm, tm), 1)
    earlier = jnp.where(col < row, 1.0, 0.0).astype(BF16)
    before = jnp.dot(earlier, onehot.astype(BF16), preferred_element_type=F32) + carry[...]
    r1 = jnp.sum(jnp.where(hit1, before, 0.0), axis=-1, keepdims=True)
    r2 = jnp.sum(jnp.where(hit2, before, 0.0), axis=-1, keepdims=True)
    two = lax.broadcasted_iota(jnp.int32, (tm, 2), 1)
    rank_ref[...] = jnp.where(two == 0, r1, r2).astype(jnp.int32)
    total = carry[...] + jnp.sum(onehot, axis=0, keepdims=True)
    carry[...] = total
    cnt_ref[...] = total.astype(jnp.int32)


def _expert_ranks(eid2d, tm=512):
    t = eid2d.shape[0]
    return pl.pallas_call(
        _rank_kernel,
        grid=(t // tm,),
        in_specs=[pl.BlockSpec((tm, 2), lambda i: (i, 0))],
        out_specs=[pl.BlockSpec((tm, 2), lambda i: (i, 0)), pl.BlockSpec((1, LANES), lambda i: (0, 0))],
        out_shape=[jax.ShapeDtypeStruct((t, 2), jnp.int32), jax.ShapeDtypeStruct((1, LANES), jnp.int32)],
        scratch_shapes=[pltpu.VMEM((1, LANES), F32)],
        compiler_params=_params("arbitrary"),
        name="expert_ranks",
    )(eid2d)


def _dispatch_kernel(dest_ref, pad_lo_ref, pad_hi_ref, h_ref, xs_ref, zrow, sem, zsem, *, tm, n_tok):
    i = pl.program_id(0)
    base = i * tm
    unroll = 8

    def row_copy(r, k):
        return pltpu.make_async_copy(h_ref.at[pl.ds(r, 1)],
                                     xs_ref.at[pl.ds(dest_ref[k * n_tok + base + r], 1)], sem)

    def issue(c, carry):
        for u in range(unroll):
            for k in range(2):
                row_copy(c * unroll + u, k).start()
        return carry

    lax.fori_loop(0, tm // unroll, issue, 0)

    @pl.when(i == 0)
    def _():
        zrow[...] = jnp.zeros_like(zrow)

        def zero_copy(s):
            return pltpu.make_async_copy(zrow, xs_ref.at[pl.ds(s, 1)], zsem)

        def start_range(e, carry):
            def start(s, c2):
                zero_copy(s).start()
                return c2

            lax.fori_loop(pad_lo_ref[e], pad_hi_ref[e], start, 0)
            return carry

        def wait_range(e, carry):
            def wait(s, c2):
                zero_copy(s).wait()
                return c2

            lax.fori_loop(pad_lo_ref[e], pad_hi_ref[e], wait, 0)
            return carry

        lax.fori_loop(0, pad_lo_ref.shape[0], start_range, 0)
        lax.fori_loop(0, pad_lo_ref.shape[0], wait_range, 0)

    def drain(c, carry):
        for u in range(unroll):
            for k in range(2):
                row_copy(c * unroll + u, k).wait()
        return carry

    lax.fori_loop(0, tm // unroll, drain, 0)


def _dispatch(dest_kmajor, pad_lo, pad_hi, h3p_2d, n_slots, tm=512):
    t, half = h3p_2d.shape
    kern = functools.partial(_dispatch_kernel, tm=tm, n_tok=t)
    return pl.pallas_call(
        kern,
        grid_spec=pltpu.PrefetchScalarGridSpec(
            num_scalar_prefetch=3,
            grid=(t // tm,),
            in_specs=[pl.BlockSpec((tm, half), lambda i, *_: (i, 0))],
            out_specs=pl.BlockSpec(memory_space=pl.ANY),
            scratch_shapes=[pltpu.VMEM((1, half), jnp.uint32), pltpu.SemaphoreType.DMA, pltpu.SemaphoreType.DMA],
        ),
        out_shape=jax.ShapeDtypeStruct((n_slots, half), jnp.uint32),
        compiler_params=pltpu.CompilerParams(dimension_semantics=("arbitrary",), has_side_effects=True),
        name="moe_dispatch",
    )(dest_kmajor, pad_lo, pad_hi, h3p_2d)


def _row_gather(src_ref, idx_ref, idx_base, n_rows, dst_ref, sem):
    for r in range(n_rows):
        pltpu.make_async_copy(src_ref.at[pl.ds(idx_ref[idx_base + r], 1)], dst_ref.at[pl.ds(r, 1)], sem).start()


def _row_gather_wait(src_ref, n_rows, dst_ref, sem):
    for r in range(n_rows):
        pltpu.make_async_copy(src_ref.at[pl.ds(0, 1)], dst_ref.at[pl.ds(r, 1)], sem).wait()


def _expert_kernel(be_ref, nused_ref, xs_ref, wgu_ref, wd_ref, ys_ref):
    del be_ref
    i = pl.program_id(0)

    @pl.when(i < nused_ref[0])
    def _():
        x_lo, x_hi = _unpack_bf16_pairs(xs_ref[...])
        half = x_lo.shape[1]
        gu = (jnp.dot(x_lo.astype(BF16), wgu_ref[0, :half, :], preferred_element_type=F32)
              + jnp.dot(x_hi.astype(BF16), wgu_ref[0, half:, :], preferred_element_type=F32))
        gate = gu[:, :D_EXPERT]
        hid = gate * jax.nn.sigmoid(gate) * gu[:, D_EXPERT:]
        y = jnp.dot(hid.astype(BF16), wd_ref[0], preferred_element_type=F32)
        ys_ref[...] = _pack_bf16_pairs(y.astype(BF16).astype(F32))

    @pl.when(i >= nused_ref[0])
    def _():
        ys_ref[...] = jnp.zeros_like(ys_ref)


def _experts(block_expert, n_used, xs, w_gu_bf, w_d_bf):
    n_slots, half = xs.shape
    d = 2 * half
    return pl.pallas_call(
        _expert_kernel,
        grid_spec=pltpu.PrefetchScalarGridSpec(
            num_scalar_prefetch=2,
            grid=(n_slots // MOE_BLOCK,),
            in_specs=[pl.BlockSpec((MOE_BLOCK, half), lambda i, be, nu: (i, 0)),
                      pl.BlockSpec((1, d, 2 * D_EXPERT), lambda i, be, nu: (be[i], 0, 0)),
                      pl.BlockSpec((1, D_EXPERT, d), lambda i, be, nu: (be[i], 0, 0))],
            out_specs=pl.BlockSpec((MOE_BLOCK, half), lambda i, be, nu: (i, 0)),
        ),
        out_shape=jax.ShapeDtypeStruct((n_slots, half), jnp.uint32),
        compiler_params=_params("arbitrary"),
        name="moe_experts",
    )(block_expert, n_used, xs, w_gu_bf, w_d_bf)


def _combine_kernel(dest_ref, x2_ref, gate_ref, g_ref, ys_ref, o_ref, ybuf, sems, *, tm):
    i = pl.program_id(0)
    n = pl.num_programs(0)
    slot = i % 2
    unroll = 8

    n_tok = n * tm

    def start_tile(tile, buf_slot):
        def issue(c, carry):
            for k in range(2):
                _row_gather(ys_ref, dest_ref, k * n_tok + tile * tm + c * unroll, unroll,
                            ybuf.at[buf_slot, k, pl.ds(c * unroll, unroll)], sems.at[buf_slot])
            return carry

        lax.fori_loop(0, tm // unroll, issue, 0)

    @pl.when(i == 0)
    def _():
        start_tile(0, 0)

    @pl.when(i + 1 < n)
    def _():
        start_tile(i + 1, 1 - slot)

    def drain(c, carry):
        for k in range(2):
            _row_gather_wait(ys_ref, unroll, ybuf.at[slot, k, pl.ds(c * unroll, unroll)], sems.at[slot])
        return carry

    lax.fori_loop(0, tm // unroll, drain, 0)
    gates = gate_ref[...]
    lo0, hi0 = _unpack_bf16_pairs(ybuf[slot, 0])
    lo1, hi1 = _unpack_bf16_pairs(ybuf[slot, 1])
    y = jnp.concatenate([gates[:, 0:1] * lo0 + gates[:, 1:2] * lo1,
                         gates[:, 0:1] * hi0 + gates[:, 1:2] * hi1], axis=-1)
    o_ref[...] = _rms(x2_ref[...] + y, g_ref[...])


def _combine(dest_flat, x2_2d, gate2d, g_final, ys, tm=256):
    t, d = x2_2d.shape
    kern = functools.partial(_combine_kernel, tm=tm)
    return pl.pallas_call(
        kern,
        grid_spec=pltpu.PrefetchScalarGridSpec(
            num_scalar_prefetch=1,
            grid=(t // tm,),
            in_specs=[pl.BlockSpec((tm, d), lambda i, dst: (i, 0)),
                      pl.BlockSpec((tm, 2), lambda i, dst: (i, 0)),
                      pl.BlockSpec((1, d), lambda i, dst: (0, 0)),
                      pl.BlockSpec(memory_space=pl.ANY)],
            out_specs=pl.BlockSpec((tm, d), lambda i, dst: (i, 0)),
            scratch_shapes=[pltpu.VMEM((2, 2, tm, d // 2), jnp.uint32), pltpu.SemaphoreType.DMA((2,))],
        ),
        out_shape=jax.ShapeDtypeStruct((t, d), F32),
        compiler_params=_params("arbitrary"),
        name="moe_combine",
    )(dest_flat, x2_2d, gate2d, g_final, ys)


def _encode(x, mem, p):
    b, seq_len, d = x.shape
    t = b * seq_len
    q, k, v, u = _in_proj(x.reshape(t, d), p["g_mix"], p["w_in"])
    shape3 = (b, seq_len, NA_WIDTH)
    a_n = _neighborhood_attention(q.reshape(shape3), k.reshape(shape3), v.reshape(shape3), p["na_bias"], p["g_branch_a"])
    p_n = _pool_mixer(u.reshape(shape3), p["w_pool"], p["pool_scale"], p["g_branch_b"])
    kv = _norm_matmul(mem.reshape(b * N_MEM, d), p["g_mem"], p["w_kv"], BF16).reshape(b, N_MEM, 2 * MEM_WIDTH)
    x2, h3p, eid, gate = _post_mix(x, a_n, p_n, p["w_out"], p["g_cross"], p["w_q"], kv[:, :, :MEM_WIDTH],
                                   kv[:, :, MEM_WIDTH:], p["w_o"], p["g_moe"], p["w_r"], p["b_r"])
    x2 = x2.reshape(t, d)
    eid2d = eid.reshape(t, 2)
    rank, counts = _expert_ranks(eid2d)
    counts = counts[0, :N_EXPERTS]
    padded = ((counts + MOE_BLOCK - 1) // MOE_BLOCK) * MOE_BLOCK
    padded_ends = jnp.cumsum(padded)
    padded_starts = padded_ends - padded
    n_blocks = -(-(2 * t + N_EXPERTS * (MOE_BLOCK - 1)) // MOE_BLOCK)
    n_slots = n_blocks * MOE_BLOCK
    block_pos = jnp.arange(n_blocks, dtype=jnp.int32) * MOE_BLOCK
    block_expert = jnp.minimum(jnp.sum(padded_ends[None, :] <= block_pos[:, None], axis=1), N_EXPERTS - 1).astype(jnp.int32)
    is_e = eid2d[:, :, None] == jnp.arange(N_EXPERTS, dtype=jnp.int32)
    dest = (jnp.sum(jnp.where(is_e, padded_starts, 0), axis=-1) + rank).astype(jnp.int32)
    dest_kmajor = dest.T.reshape(-1)
    pad_lo = jnp.concatenate([padded_starts + counts, padded_ends[-1:]]).astype(jnp.int32)
    pad_hi = jnp.concatenate([padded_ends, jnp.full((1,), n_slots, jnp.int32)]).astype(jnp.int32)
    n_used = (padded_ends[-1:] // MOE_BLOCK).astype(jnp.int32)
    xs = _dispatch(dest_kmajor, pad_lo, pad_hi, h3p.reshape(t, d // 2), n_slots)
    ys = _experts(block_expert, n_used, xs, p["w_gu"], p["w_d"])
    y = _combine(dest_kmajor, x2, gate.reshape(t, 2), p["g_final"], ys)
    return y.reshape(b, seq_len, d)


def kernel(x_prompt, x_sample, mem_prompt, mem_sample, g_mix, w_in, rpb, w_pool, pool_scale, g_branch_a, g_branch_b, w_out, g_cross, g_mem, w_q_mem, w_kv_mem, w_o_mem, g_moe, w_coarse, b_coarse, w_fine, b_fine, w_gate_e, w_up_e, w_down_e, g_final):
    depth = w_in.shape[0]
    xp, xs = x_prompt, x_sample
    row = lambda a: a.reshape(1, -1).astype(F32)
    for l in range(depth):
        w_r = jnp.concatenate([w_coarse[l].T, w_fine[l].T], axis=0).astype(F32)
        w_r = jnp.pad(w_r, ((0, LANES - w_r.shape[0]), (0, 0)))
        b_r = jnp.concatenate([b_coarse[l], b_fine[l]]).astype(F32)
        p = {
            "g_mix": row(g_mix[l]), "w_in": w_in[l].astype(BF16),
            "na_bias": _na_bias_table(rpb[l]), "g_branch_a": row(g_branch_a[l]),
            "w_pool": w_pool[l].astype(BF16), "pool_scale": row(pool_scale[l]), "g_branch_b": row(g_branch_b[l]),
            "w_out": w_out[l].astype(BF16), "g_cross": row(g_cross[l]), "g_mem": row(g_mem[l]),
            "w_q": w_q_mem[l].astype(BF16), "w_kv": w_kv_mem[l].astype(BF16), "w_o": w_o_mem[l].astype(BF16),
            "g_moe": row(g_moe[l]),
            "w_r": w_r.astype(BF16),
            "b_r": jnp.pad(b_r, (0, LANES - b_r.shape[0])).reshape(1, LANES),
            "w_gu": jnp.concatenate([w_gate_e[l], w_up_e[l]], axis=-1).astype(BF16),
            "w_d": w_down_e[l].astype(BF16),
            "g_final": row(g_final),
        }
        assert depth == 1, "final-norm fusion assumes a single layer"
        xp = _encode(xp, mem_prompt, p)
        xs = _encode(xs, mem_sample, p)
    return (xp, xs)
```

```python
import functools

import jax
import jax.numpy as jnp
from jax import lax
from jax.experimental import pallas as pl
from jax.experimental.pallas import tpu as pltpu

F32 = jnp.float32
BF16 = jnp.bfloat16

D_MODEL = 2048
GRID_W = 64
NA_HEADS = 16
NA_WIDTH = 1024
NA_HEAD_DIM = 64
WIN_H = 8
WIN_W = 16
POOL_WINDOWS = (2, 4, 8, 16)
POOL_WIDTH = 1024
POOL_GROUP_DIM = 256
POOL_HALO = 16
N_MEM = 256
MEM_HEADS = 4
MEM_HEAD_DIM = 128
MEM_WIDTH = 512
N_EXPERT_GROUPS = 4
EXPERTS_PER_GROUP = 8
N_EXPERTS = 32
D_EXPERT = 512
MOE_BLOCK = 256
EPS = 1e-6
NEG = -1e30
LANES = 128
ROW_GROUP = 8
KV_CHUNK_ROWS = 4
SOFTMAX_ROWS = 16
VMEM_LIMIT = 56 * 1024 * 1024


def _rms(x, g):
    return x * lax.rsqrt(jnp.mean(x * x, axis=-1, keepdims=True) + EPS) * g


def _pack_bf16_pairs(x):
    n = x.shape[1] // 2
    bits = lax.bitcast_convert_type(x, jnp.uint32)
    return (bits[:, :n] >> 16) | (bits[:, n:] & jnp.uint32(0xFFFF0000))


def _unpack_bf16_pairs(u):
    lo = lax.bitcast_convert_type(u << 16, F32)
    hi = lax.bitcast_convert_type(u & jnp.uint32(0xFFFF0000), F32)
    return lo, hi


def _const_spec(shape):
    nd = len(shape)
    return pl.BlockSpec(shape, lambda *_: (0,) * nd, pipeline_mode=pl.Buffered(1))


def _params(*sem):
    return pltpu.CompilerParams(dimension_semantics=sem, vmem_limit_bytes=VMEM_LIMIT)


def _in_proj_kernel(x_ref, g_ref, w_ref, q_ref, k_ref, v_ref, u_ref):
    h = _rms(x_ref[...], g_ref[...]).astype(BF16)
    q_ref[...] = (jnp.dot(h, w_ref[:, 0:NA_WIDTH], preferred_element_type=F32)
                  * (NA_HEAD_DIM ** -0.5)).astype(BF16)
    k_ref[...] = jnp.dot(h, w_ref[:, NA_WIDTH:2 * NA_WIDTH], preferred_element_type=F32).astype(BF16)
    v_ref[...] = jnp.dot(h, w_ref[:, 2 * NA_WIDTH:3 * NA_WIDTH], preferred_element_type=F32).astype(BF16)
    u_ref[...] = jnp.dot(h, w_ref[:, 3 * NA_WIDTH:], preferred_element_type=F32)


def _in_proj(x2d, g_mix, w_in_bf, tm=512):
    t = x2d.shape[0]
    row = lambda i: (i, 0)
    return pl.pallas_call(
        _in_proj_kernel,
        grid=(t // tm,),
        in_specs=[pl.BlockSpec((tm, D_MODEL), row), _const_spec((1, D_MODEL)),
                  _const_spec(w_in_bf.shape)],
        out_specs=[pl.BlockSpec((tm, NA_WIDTH), row)] * 4,
        out_shape=[jax.ShapeDtypeStruct((t, NA_WIDTH), BF16)] * 3
        + [jax.ShapeDtypeStruct((t, POOL_WIDTH), F32)],
        compiler_params=_params("parallel"),
        name="in_proj",
    )(x2d, g_mix, w_in_bf)


def _norm_matmul_kernel(x_ref, g_ref, w_ref, o_ref):
    h = _rms(x_ref[...], g_ref[...]).astype(BF16)
    o_ref[...] = jnp.dot(h, w_ref[...], preferred_element_type=F32).astype(o_ref.dtype)


def _norm_matmul(x2d, g, w_bf, out_dtype, tm=256):
    t, k = x2d.shape
    n = w_bf.shape[1]
    return pl.pallas_call(
        _norm_matmul_kernel,
        grid=(t // tm,),
        in_specs=[pl.BlockSpec((tm, k), lambda i: (i, 0)), _const_spec((1, k)), _const_spec((k, n))],
        out_specs=pl.BlockSpec((tm, n), lambda i: (i, 0)),
        out_shape=jax.ShapeDtypeStruct((t, n), out_dtype),
        compiler_params=_params("parallel"),
        name="norm_matmul",
    )(x2d, g, w_bf)


def _pool_kernel(prev_ref, cur_ref, next_ref, w_ref, scale_ref, g_ref, o_ref, ext_ref, mix_ref, *, seq_len, tm):
    i = pl.program_id(1)
    n_i = pl.num_programs(1)
    ext_ref[0:POOL_HALO, :] = jnp.where(i > 0, prev_ref[0], 0.0)
    ext_ref[POOL_HALO:POOL_HALO + tm, :] = cur_ref[0]
    ext_ref[POOL_HALO + tm:, :] = jnp.where(i < n_i - 1, next_ref[0], 0.0)
    n_ext = tm + 2 * POOL_HALO
    pos = i * tm + lax.broadcasted_iota(jnp.int32, (tm, 1), 0)
    for gi, w in enumerate(POOL_WINDOWS):
        cols = slice(gi * POOL_GROUP_DIM, (gi + 1) * POOL_GROUP_DIM)
        e = ext_ref[:, cols]
        s = e + pltpu.roll(e, 1, 0)
        half = 1
        while 2 * half < w:
            s = pltpu.roll(s, half, 0) + pltpu.roll(s, n_ext - half, 0)
            half *= 2
        lo = jnp.maximum(pos - w // 2, 0)
        hi = jnp.minimum(pos - w // 2 + w, seq_len)
        cnt = (hi - lo).astype(F32)
        pooled = s[POOL_HALO:POOL_HALO + tm] / cnt - e[POOL_HALO:POOL_HALO + tm]
        mix_ref[:, cols] = jnp.dot(pooled.astype(BF16), w_ref[gi], preferred_element_type=F32)
    mixed = mix_ref[...] * scale_ref[...]
    o_ref[0] = _rms(mixed, g_ref[...]).astype(BF16)


def _pool_mixer(u3d, w_pool_bf, pool_scale, g_b, tm=512):
    b, seq_len, _ = u3d.shape
    tm = min(tm, seq_len)
    hb = tm // POOL_HALO
    n_hb = seq_len // POOL_HALO
    kern = functools.partial(_pool_kernel, seq_len=seq_len, tm=tm)
    return pl.pallas_call(
        kern,
        grid=(b, seq_len // tm),
        in_specs=[
            pl.BlockSpec((1, POOL_HALO, POOL_WIDTH), lambda bi, i: (bi, jnp.maximum(i * hb - 1, 0), 0)),
            pl.BlockSpec((1, tm, POOL_WIDTH), lambda bi, i: (bi, i, 0)),
            pl.BlockSpec((1, POOL_HALO, POOL_WIDTH), lambda bi, i: (bi, jnp.minimum((i + 1) * hb, n_hb - 1), 0)),
            _const_spec(w_pool_bf.shape), _const_spec((1, POOL_WIDTH)), _const_spec((1, POOL_WIDTH)),
        ],
        out_specs=pl.BlockSpec((1, tm, POOL_WIDTH), lambda bi, i: (bi, i, 0)),
        out_shape=jax.ShapeDtypeStruct((b, seq_len, POOL_WIDTH), BF16),
        scratch_shapes=[pltpu.VMEM((tm + 2 * POOL_HALO, POOL_WIDTH), F32), pltpu.VMEM((tm, POOL_WIDTH), F32)],
        compiler_params=_params("parallel", "parallel"),
        name="pool_mixer",
    )(u3d, u3d, u3d, w_pool_bf, pool_scale, g_b)


def _na_bias_table(rpb):
    c = jnp.arange(GRID_W)[:, None]
    kc = jnp.arange(GRID_W)[None, :]
    start = jnp.clip(c - WIN_W // 2, 0, GRID_W - WIN_W)
    valid = (kc >= start) & (kc < start + WIN_W)
    dc = kc - c + (WIN_W - 1)
    onehot = (dc[:, :, None] == jnp.arange(2 * WIN_W - 1)[None, None, :]).astype(F32)
    band = jnp.einsum("hrd,cqd->hrcq", rpb.astype(F32), onehot, precision=lax.Precision.HIGHEST)
    band = jnp.where(valid[None, None], band, NEG)
    n_dr = 2 * WIN_H - 2
    units = jnp.concatenate([band[:, :n_dr], band[:, 1:n_dr + 1]], axis=-1)
    units = units.reshape(NA_HEADS // 2, 2, n_dr, GRID_W, 2 * GRID_W)
    return jnp.transpose(units, (0, 2, 1, 3, 4)).reshape(NA_HEADS // 2, n_dr, 2 * GRID_W, 2 * GRID_W)


def _na_kernel(q_ref, k0, k1, k2, k3, v0, v1, v2, v3, bias_ref, g_ref, o_ref, kbuf, vbuf, acc, s_scr, p_scr, linv_scr,
               *, n_rows):
    g = pl.program_id(1)
    chunk = KV_CHUNK_ROWS * GRID_W
    for c, (kr, vr) in enumerate(((k0, v0), (k1, v1), (k2, v2), (k3, v3))):
        kbuf[c * chunk:(c + 1) * chunk, :] = kr[0]
        vbuf[c * chunk:(c + 1) * chunk, :] = vr[0]
    first_head = lax.broadcasted_iota(jnp.int32, (GRID_W, LANES), 1) < NA_HEAD_DIM
    n_keys = WIN_H * GRID_W

    n_pairs = NA_HEADS // 2

    def row_params(j):
        r = g * ROW_GROUP + j
        r0 = jnp.clip(r - WIN_H // 2, 0, n_rows - WIN_H)
        pat = r - r0
        koff = pl.multiple_of((r0 - (g * ROW_GROUP - KV_CHUNK_ROWS)) * GRID_W, GRID_W)
        qoff = pl.multiple_of(j * GRID_W, GRID_W)
        return pat, koff, qoff

    def scores(slot, prm):
        _, koff, qoff = prm
        for hp in range(n_pairs):
            cols = slice(hp * LANES, (hp + 1) * LANES)
            q2 = q_ref[0, pl.ds(qoff, GRID_W), cols]
            zero = jnp.zeros_like(q2)
            qq = jnp.concatenate([jnp.where(first_head, q2, zero), jnp.where(first_head, zero, q2)], axis=0)
            kk = kbuf[pl.ds(koff, n_keys), cols]
            s_scr[slot, hp] = lax.dot_general(qq, kk, (((1,), (1,)), ((), ())), preferred_element_type=F32)

    def softmax(slot, prm):
        pat = prm[0]
        for hp in range(n_pairs):
            for rc in range(2 * GRID_W // SOFTMAX_ROWS):
                rows = slice(rc * SOFTMAX_ROWS, (rc + 1) * SOFTMAX_ROWS)
                bias = jnp.concatenate([bias_ref[hp, 2 * u - pat + (WIN_H - 1), rows, :] for u in range(WIN_H // 2)],
                                       axis=-1)
                s = s_scr[slot, hp, rows, :] + bias
                p = jnp.exp(s - jnp.max(s, axis=-1, keepdims=True))
                p_scr[slot, hp, rows, :] = p.astype(BF16)
                linv = 1.0 / jnp.sum(p, axis=-1, keepdims=True)
                linv_scr[slot, hp, rows, :] = jnp.broadcast_to(linv, (SOFTMAX_ROWS, LANES))

    def outputs(slot, prm):
        _, koff, qoff = prm
        for hp in range(n_pairs):
            cols = slice(hp * LANES, (hp + 1) * LANES)
            vv = vbuf[pl.ds(koff, n_keys), cols]
            o = jnp.dot(p_scr[slot, hp], vv, preferred_element_type=F32) * linv_scr[slot, hp]
            acc[pl.ds(qoff, GRID_W), cols] = jnp.where(first_head, o[:GRID_W], o[GRID_W:])

    def row_pair_body(jj, carry):
        prms = [row_params(2 * jj + s) for s in range(2)]
        for phase in (scores, softmax, outputs):
            for s in range(2):
                phase(s, prms[s])
        return carry

    lax.fori_loop(0, ROW_GROUP // 2, row_pair_body, 0)
    o_ref[0] = _rms(acc[...], g_ref[...]).astype(BF16)


def _neighborhood_attention(q3d, k3d, v3d, bias_tbl, g_a):
    b, seq_len, _ = q3d.shape
    n_rows = seq_len // GRID_W
    tq = ROW_GROUP * GRID_W
    chunk = KV_CHUNK_ROWS * GRID_W
    n_chunks = seq_len // chunk

    def kv_spec(c):
        return pl.BlockSpec((1, chunk, NA_WIDTH),
                            lambda bi, g: (bi, jnp.clip(2 * g - 1 + c, 0, n_chunks - 1), 0))

    kern = functools.partial(_na_kernel, n_rows=n_rows)
    return pl.pallas_call(
        kern,
        grid=(b, n_rows // ROW_GROUP),
        in_specs=[pl.BlockSpec((1, tq, NA_WIDTH), lambda bi, g: (bi, g, 0))]
        + [kv_spec(c) for c in range(4)] * 2
        + [_const_spec(bias_tbl.shape), _const_spec((1, NA_WIDTH))],
        out_specs=pl.BlockSpec((1, tq, NA_WIDTH), lambda bi, g: (bi, g, 0)),
        out_shape=jax.ShapeDtypeStruct((b, seq_len, NA_WIDTH), BF16),
        scratch_shapes=[pltpu.VMEM((4 * chunk, NA_WIDTH), BF16), pltpu.VMEM((4 * chunk, NA_WIDTH), BF16),
                        pltpu.VMEM((tq, NA_WIDTH), F32),
                        pltpu.VMEM((2, NA_HEADS // 2, 2 * GRID_W, WIN_H * GRID_W), F32),
                        pltpu.VMEM((2, NA_HEADS // 2, 2 * GRID_W, WIN_H * GRID_W), BF16),
                        pltpu.VMEM((2, NA_HEADS // 2, 2 * GRID_W, LANES), F32)],
        compiler_params=_params("parallel", "parallel"),
        name="neighborhood_attention",
    )(q3d, k3d, k3d, k3d, k3d, v3d, v3d, v3d, v3d, bias_tbl, g_a)


def _post_kernel(x_ref, a_ref, p_ref, wout_ref, gc_ref, wq_ref, km_ref, vm_ref, wo_ref, gm_ref,
                 wr_ref, br_ref, x2_ref, h3p_ref, eid_ref, gate_ref):
    tm = x_ref.shape[1]
    x1 = (x_ref[0]
          + jnp.dot(a_ref[0], wout_ref[0:NA_WIDTH, :], preferred_element_type=F32)
          + jnp.dot(p_ref[0], wout_ref[NA_WIDTH:, :], preferred_element_type=F32))
    h2 = _rms(x1, gc_ref[...]).astype(BF16)
    qm = jnp.dot(h2, wq_ref[...], preferred_element_type=F32).astype(BF16)
    heads = []
    for hd in range(MEM_HEADS):
        cols = slice(hd * MEM_HEAD_DIM, (hd + 1) * MEM_HEAD_DIM)
        s = lax.dot_general(qm[:, cols], km_ref[0, :, cols], (((1,), (1,)), ((), ())),
                            preferred_element_type=F32) * (MEM_HEAD_DIM ** -0.5)
        m = jnp.max(s, axis=-1, keepdims=True)
        p = jnp.exp(s - m)
        l = jnp.sum(p, axis=-1, keepdims=True)
        o = jnp.dot(p.astype(BF16), vm_ref[0, :, cols], preferred_element_type=F32) / l
        heads.append(o.astype(BF16))
    o_all = jnp.concatenate(heads, axis=-1)
    x2 = x1 + jnp.dot(o_all, wo_ref[...], preferred_element_type=F32)
    x2_ref[0] = x2
    h3 = _rms(x2, gm_ref[...])

    h_bf = h3.astype(BF16)
    h3p_ref[0] = _pack_bf16_pairs(h_bf.astype(F32))
    logits = lax.dot_general(h_bf, wr_ref[...], (((1,), (1,)), ((), ())), preferred_element_type=F32) + br_ref[...]

    lane = lax.broadcasted_iota(jnp.int32, (tm, LANES), 1)
    lane_f = lane.astype(F32)
    big = float(LANES)
    cmask = lane < N_EXPERT_GROUPS
    lc = jnp.where(cmask, logits, NEG)
    cmax = jnp.max(lc, axis=-1, keepdims=True)
    g_idx = jnp.min(jnp.where(lc == cmax, lane_f, big), axis=-1, keepdims=True)
    g_gate = 1.0 / jnp.sum(jnp.exp(lc - cmax), axis=-1, keepdims=True)
    grp_f = ((lane - N_EXPERT_GROUPS) >> 3).astype(F32)
    in_fine = jnp.where(lane >= N_EXPERT_GROUPS, jnp.where(lane < N_EXPERT_GROUPS + N_EXPERTS, grp_f, -1.0), -1.0)
    lf = jnp.where(in_fine == g_idx, logits, NEG)
    v1 = jnp.max(lf, axis=-1, keepdims=True)
    i1 = jnp.min(jnp.where(lf == v1, lane_f, big), axis=-1, keepdims=True)
    lf2 = jnp.where(lane_f == i1, NEG, lf)
    v2 = jnp.max(lf2, axis=-1, keepdims=True)
    i2 = jnp.min(jnp.where(lf2 == v2, lane_f, big), axis=-1, keepdims=True)
    e21 = jnp.exp(v2 - v1)
    den = 1.0 / (1.0 + e21)
    two = lax.broadcasted_iota(jnp.int32, (tm, 2), 1)
    eid_ref[0] = jnp.where(two == 0, i1, i2).astype(jnp.int32) - N_EXPERT_GROUPS
    gate_ref[0] = jnp.where(two == 0, g_gate * den, g_gate * (e21 * den))


def _post_mix(x3d, a_n, p_n, w_out_bf, g_cross, w_q_bf, k_mem, v_mem, w_o_bf, g_moe, w_r_bf, b_r, tm=512):
    b, seq_len, _ = x3d.shape
    tok = lambda w: pl.BlockSpec((1, tm, w), lambda bi, i: (bi, i, 0))
    mem = pl.BlockSpec((1, N_MEM, MEM_WIDTH), lambda bi, i: (bi, 0, 0))
    return pl.pallas_call(
        _post_kernel,
        grid=(b, seq_len // tm),
        in_specs=[tok(D_MODEL), tok(NA_WIDTH), tok(POOL_WIDTH), _const_spec(w_out_bf.shape),
                  _const_spec((1, D_MODEL)), _const_spec(w_q_bf.shape), mem, mem, _const_spec(w_o_bf.shape),
                  _const_spec((1, D_MODEL)), _const_spec(w_r_bf.shape),
                  _const_spec((1, LANES))],
        out_specs=[tok(D_MODEL), tok(D_MODEL // 2), tok(2), tok(2)],
        out_shape=[jax.ShapeDtypeStruct((b, seq_len, D_MODEL), F32),
                   jax.ShapeDtypeStruct((b, seq_len, D_MODEL // 2), jnp.uint32),
                   jax.ShapeDtypeStruct((b, seq_len, 2), jnp.int32), jax.ShapeDtypeStruct((b, seq_len, 2), F32)],
        compiler_params=_params("parallel", "parallel"),
        name="post_mix",
    )(x3d, a_n, p_n, w_out_bf, g_cross, w_q_bf, k_mem, v_mem, w_o_bf, g_moe, w_r_bf, b_r)


def _rank_kernel(eid_ref, rank_ref, cnt_ref, carry):
    tm = eid_ref.shape[0]

    @pl.when(pl.program_id(0) == 0)
    def _():
        carry[...] = jnp.zeros_like(carry)

    lane = lax.broadcasted_iota(jnp.int32, (tm, LANES), 1)
    e1 = eid_ref[:, 0:1]
    e2 = eid_ref[:, 1:2]
    hit1 = lane == e1
    hit2 = lane == e2
    onehot = jnp.where(hit1, 1.0, 0.0) + jnp.where(hit2, 1.0, 0.0)
    row = lax.broadcasted_iota(jnp.int32, (tm, tm), 0)
    col = lax.broadcasted_iota(jnp.int32, (tm, tm), 1)
    earlier = jnp.where(col < row, 1.0, 0.0).astype(BF16)
    before = jnp.dot(earlier, onehot.astype(BF16), preferred_element_type=F32) + carry[...]
    r1 = jnp.sum(jnp.where(hit1, before, 0.0), axis=-1, keepdims=True)
    r2 = jnp.sum(jnp.where(hit2, before, 0.0), axis=-1, keepdims=True)
    two = lax.broadcasted_iota(jnp.int32, (tm, 2), 1)
    rank_ref[...] = jnp.where(two == 0, r1, r2).astype(jnp.int32)
    total = carry[...] + jnp.sum(onehot, axis=0, keepdims=True)
    carry[...] = total
    cnt_ref[...] = total.astype(jnp.int32)


def _expert_ranks(eid2d, tm=512):
    t = eid2d.shape[0]
    return pl.pallas_call(
        _rank_kernel,
        grid=(t // tm,),
        in_specs=[pl.BlockSpec((tm, 2), lambda i: (i, 0))],
        out_specs=[pl.BlockSpec((tm, 2), lambda i: (i, 0)), pl.BlockSpec((1, LANES), lambda i: (0, 0))],
        out_shape=[jax.ShapeDtypeStruct((t, 2), jnp.int32), jax.ShapeDtypeStruct((1, LANES), jnp.int32)],
        scratch_shapes=[pltpu.VMEM((1, LANES), F32)],
        compiler_params=_params("arbitrary"),
        name="expert_ranks",
    )(eid2d)


def _dispatch_kernel(dest_ref, pad_lo_ref, pad_hi_ref, h_ref, xs_ref, zrow, sem, zsem, *, tm, n_tok):
    i = pl.program_id(0)
    base = i * tm
    unroll = 8

    def row_copy(r, k):
        return pltpu.make_async_copy(h_ref.at[pl.ds(r, 1)],
                                     xs_ref.at[pl.ds(dest_ref[k * n_tok + base + r], 1)], sem)

    def issue(c, carry):
        for u in range(unroll):
            for k in range(2):
                row_copy(c * unroll + u, k).start()
        return carry

    lax.fori_loop(0, tm // unroll, issue, 0)

    @pl.when(i == 0)
    def _():
        zrow[...] = jnp.zeros_like(zrow)

        def zero_copies(wait):
            def chunk(pos, size):
                cp = pltpu.make_async_copy(zrow.at[pl.ds(0, size)], xs_ref.at[pl.ds(pos, size)], zsem)
                cp.wait() if wait else cp.start()

            def segment(e, carry):
                lo = pad_lo_ref[e]
                length = pad_hi_ref[e] - lo
                head = length & 7

                def single(r, c2):
                    chunk(lo + r, 1)
                    return c2

                lax.fori_loop(0, head, single, 0)
                pos = lo + head
                size = MOE_BLOCK // 2
                while size >= 8:
                    bit = length & size
                    pl.when(bit != 0)(functools.partial(chunk, pl.multiple_of(pos, 8), size))
                    pos = pos + bit
                    size //= 2
                return carry

            lax.fori_loop(0, N_EXPERTS, segment, 0)

            def tail(c, carry):
                chunk(pl.multiple_of(pad_lo_ref[N_EXPERTS] + c * MOE_BLOCK, MOE_BLOCK), MOE_BLOCK)
                return carry

            lax.fori_loop(0, (pad_hi_ref[N_EXPERTS] - pad_lo_ref[N_EXPERTS]) // MOE_BLOCK, tail, 0)

        zero_copies(wait=False)
        zero_copies(wait=True)

    def drain(c, carry):
        for u in range(unroll):
            for k in range(2):
                row_copy(c * unroll + u, k).wait()
        return carry

    lax.fori_loop(0, tm // unroll, drain, 0)


def _dispatch(dest_kmajor, pad_lo, pad_hi, h3p_2d, n_slots, tm=512):
    t, half = h3p_2d.shape
    kern = functools.partial(_dispatch_kernel, tm=tm, n_tok=t)
    return pl.pallas_call(
        kern,
        grid_spec=pltpu.PrefetchScalarGridSpec(
            num_scalar_prefetch=3,
            grid=(t // tm,),
            in_specs=[pl.BlockSpec((tm, half), lambda i, *_: (i, 0))],
            out_specs=pl.BlockSpec(memory_space=pl.ANY),
            scratch_shapes=[pltpu.VMEM((MOE_BLOCK, half), jnp.uint32), pltpu.SemaphoreType.DMA,
                            pltpu.SemaphoreType.DMA],
        ),
        out_shape=jax.ShapeDtypeStruct((n_slots, half), jnp.uint32),
        compiler_params=pltpu.CompilerParams(dimension_semantics=("arbitrary",), has_side_effects=True),
        name="moe_dispatch",
    )(dest_kmajor, pad_lo, pad_hi, h3p_2d)


def _row_gather(src_ref, idx_ref, idx_base, n_rows, dst_ref, sem):
    for r in range(n_rows):
        pltpu.make_async_copy(src_ref.at[pl.ds(idx_ref[idx_base + r], 1)], dst_ref.at[pl.ds(r, 1)], sem).start()


def _row_gather_wait(src_ref, n_rows, dst_ref, sem):
    for r in range(n_rows):
        pltpu.make_async_copy(src_ref.at[pl.ds(0, 1)], dst_ref.at[pl.ds(r, 1)], sem).wait()


def _expert_kernel(be_ref, nused_ref, xs_ref, wgu_ref, wd_ref, ys_ref):
    del be_ref
    i = pl.program_id(0)

    @pl.when(i < nused_ref[0])
    def _():
        x_lo, x_hi = _unpack_bf16_pairs(xs_ref[...])
        half = x_lo.shape[1]
        gu = (jnp.dot(x_lo.astype(BF16), wgu_ref[0, :half, :], preferred_element_type=F32)
              + jnp.dot(x_hi.astype(BF16), wgu_ref[0, half:, :], preferred_element_type=F32))
        gate = gu[:, :D_EXPERT]
        hid = gate * jax.nn.sigmoid(gate) * gu[:, D_EXPERT:]
        y = jnp.dot(hid.astype(BF16), wd_ref[0], preferred_element_type=F32)
        ys_ref[...] = _pack_bf16_pairs(y.astype(BF16).astype(F32))

    @pl.when(i >= nused_ref[0])
    def _():
        ys_ref[...] = jnp.zeros_like(ys_ref)


def _experts(block_expert, n_used, xs, w_gu_bf, w_d_bf):
    n_slots, half = xs.shape
    d = 2 * half
    return pl.pallas_call(
        _expert_kernel,
        grid_spec=pltpu.PrefetchScalarGridSpec(
            num_scalar_prefetch=2,
            grid=(n_slots // MOE_BLOCK,),
            in_specs=[pl.BlockSpec((MOE_BLOCK, half), lambda i, be, nu: (i, 0)),
                      pl.BlockSpec((1, d, 2 * D_EXPERT), lambda i, be, nu: (be[i], 0, 0)),
                      pl.BlockSpec((1, D_EXPERT, d), lambda i, be, nu: (be[i], 0, 0))],
            out_specs=pl.BlockSpec((MOE_BLOCK, half), lambda i, be, nu: (i, 0)),
        ),
        out_shape=jax.ShapeDtypeStruct((n_slots, half), jnp.uint32),
        compiler_params=_params("arbitrary"),
        name="moe_experts",
    )(block_expert, n_used, xs, w_gu_bf, w_d_bf)


def _combine_kernel(dest_ref, x2_ref, gate_ref, g_ref, ys_ref, o_ref, ybuf, sems, *, tm):
    i = pl.program_id(0)
    n = pl.num_programs(0)
    slot = i % 2
    unroll = 8

    n_tok = n * tm

    def start_tile(tile, buf_slot):
        def issue(c, carry):
            for k in range(2):
                _row_gather(ys_ref, dest_ref, k * n_tok + tile * tm + c * unroll, unroll,
                            ybuf.at[buf_slot, k, pl.ds(c * unroll, unroll)], sems.at[buf_slot])
            return carry

        lax.fori_loop(0, tm // unroll, issue, 0)

    @pl.when(i == 0)
    def _():
        start_tile(0, 0)

    @pl.when(i + 1 < n)
    def _():
        start_tile(i + 1, 1 - slot)

    def drain(c, carry):
        for k in range(2):
            _row_gather_wait(ys_ref, unroll, ybuf.at[slot, k, pl.ds(c * unroll, unroll)], sems.at[slot])
        return carry

    lax.fori_loop(0, tm // unroll, drain, 0)
    gates = gate_ref[...]
    lo0, hi0 = _unpack_bf16_pairs(ybuf[slot, 0])
    lo1, hi1 = _unpack_bf16_pairs(ybuf[slot, 1])
    y = jnp.concatenate([gates[:, 0:1] * lo0 + gates[:, 1:2] * lo1,
                         gates[:, 0:1] * hi0 + gates[:, 1:2] * hi1], axis=-1)
    o_ref[...] = _rms(x2_ref[...] + y, g_ref[...])


def _combine(dest_flat, x2_2d, gate2d, g_final, ys, tm=512):
    t, d = x2_2d.shape
    kern = functools.partial(_combine_kernel, tm=tm)
    return pl.pallas_call(
        kern,
        grid_spec=pltpu.PrefetchScalarGridSpec(
            num_scalar_prefetch=1,
            grid=(t // tm,),
            in_specs=[pl.BlockSpec((tm, d), lambda i, dst: (i, 0)),
                      pl.BlockSpec((tm, 2), lambda i, dst: (i, 0)),
                      pl.BlockSpec((1, d), lambda i, dst: (0, 0)),
                      pl.BlockSpec(memory_space=pl.ANY)],
            out_specs=pl.BlockSpec((tm, d), lambda i, dst: (i, 0)),
            scratch_shapes=[pltpu.VMEM((2, 2, tm, d // 2), jnp.uint32), pltpu.SemaphoreType.DMA((2,))],
        ),
        out_shape=jax.ShapeDtypeStruct((t, d), F32),
        compiler_params=_params("arbitrary"),
        name="moe_combine",
    )(dest_flat, x2_2d, gate2d, g_final, ys)


def _encode(x, mem, p):
    b, seq_len, d = x.shape
    t = b * seq_len
    q, k, v, u = _in_proj(x.reshape(t, d), p["g_mix"], p["w_in"])
    shape3 = (b, seq_len, NA_WIDTH)
    a_n = _neighborhood_attention(q.reshape(shape3), k.reshape(shape3), v.reshape(shape3), p["na_bias"], p["g_branch_a"])
    p_n = _pool_mixer(u.reshape(shape3), p["w_pool"], p["pool_scale"], p["g_branch_b"])
    kv = _norm_matmul(mem.reshape(b * N_MEM, d), p["g_mem"], p["w_kv"], BF16).reshape(b, N_MEM, 2 * MEM_WIDTH)
    x2, h3p, eid, gate = _post_mix(x, a_n, p_n, p["w_out"], p["g_cross"], p["w_q"], kv[:, :, :MEM_WIDTH],
                                   kv[:, :, MEM_WIDTH:], p["w_o"], p["g_moe"], p["w_r"], p["b_r"])
    x2 = x2.reshape(t, d)
    eid2d = eid.reshape(t, 2)
    rank, counts = _expert_ranks(eid2d)
    counts = counts[0, :N_EXPERTS]
    padded = ((counts + MOE_BLOCK - 1) // MOE_BLOCK) * MOE_BLOCK
    padded_ends = jnp.cumsum(padded)
    padded_starts = padded_ends - padded
    n_blocks = -(-(2 * t + N_EXPERTS * (MOE_BLOCK - 1)) // MOE_BLOCK)
    n_slots = n_blocks * MOE_BLOCK
    block_pos = jnp.arange(n_blocks, dtype=jnp.int32) * MOE_BLOCK
    block_expert = jnp.minimum(jnp.sum(padded_ends[None, :] <= block_pos[:, None], axis=1), N_EXPERTS - 1).astype(jnp.int32)
    is_e = eid2d[:, :, None] == jnp.arange(N_EXPERTS, dtype=jnp.int32)
    dest = (jnp.sum(jnp.where(is_e, padded_starts, 0), axis=-1) + rank).astype(jnp.int32)
    dest_kmajor = dest.T.reshape(-1)
    pad_lo = jnp.concatenate([padded_starts + counts, padded_ends[-1:]]).astype(jnp.int32)
    pad_hi = jnp.concatenate([padded_ends, jnp.full((1,), n_slots, jnp.int32)]).astype(jnp.int32)
    n_used = (padded_ends[-1:] // MOE_BLOCK).astype(jnp.int32)
    xs = _dispatch(dest_kmajor, pad_lo, pad_hi, h3p.reshape(t, d // 2), n_slots)
    ys = _experts(block_expert, n_used, xs, p["w_gu"], p["w_d"])
    y = _combine(dest_kmajor, x2, gate.reshape(t, 2), p["g_final"], ys)
    return y.reshape(b, seq_len, d)


def kernel(x_prompt, x_sample, mem_prompt, mem_sample, g_mix, w_in, rpb, w_pool, pool_scale, g_branch_a, g_branch_b, w_out, g_cross, g_mem, w_q_mem, w_kv_mem, w_o_mem, g_moe, w_coarse, b_coarse, w_fine, b_fine, w_gate_e, w_up_e, w_down_e, g_final):
    depth = w_in.shape[0]
    xp, xs = x_prompt, x_sample
    row = lambda a: a.reshape(1, -1).astype(F32)
    for l in range(depth):
        w_r = jnp.concatenate([w_coarse[l].T, w_fine[l].T], axis=0).astype(F32)
        w_r = jnp.pad(w_r, ((0, LANES - w_r.shape[0]), (0, 0)))
        b_r = jnp.concatenate([b_coarse[l], b_fine[l]]).astype(F32)
        p = {
            "g_mix": row(g_mix[l]), "w_in": w_in[l].astype(BF16),
            "na_bias": _na_bias_table(rpb[l]), "g_branch_a": row(g_branch_a[l]),
            "w_pool": w_pool[l].astype(BF16), "pool_scale": row(pool_scale[l]), "g_branch_b": row(g_branch_b[l]),
            "w_out": w_out[l].astype(BF16), "g_cross": row(g_cross[l]), "g_mem": row(g_mem[l]),
            "w_q": w_q_mem[l].astype(BF16), "w_kv": w_kv_mem[l].astype(BF16), "w_o": w_o_mem[l].astype(BF16),
            "g_moe": row(g_moe[l]),
            "w_r": w_r.astype(BF16),
            "b_r": jnp.pad(b_r, (0, LANES - b_r.shape[0])).reshape(1, LANES),
            "w_gu": jnp.concatenate([w_gate_e[l], w_up_e[l]], axis=-1).astype(BF16),
            "w_d": w_down_e[l].astype(BF16),
            "g_final": row(g_final),
        }
        assert depth == 1, "final-norm fusion assumes a single layer"
        xp = _encode(xp, mem_prompt, p)
        xs = _encode(xs, mem_sample, p)
    return (xp, xs)
```

```python
import functools

import jax
import jax.numpy as jnp
from jax import lax
from jax.experimental import pallas as pl
from jax.experimental.pallas import tpu as pltpu

F32 = jnp.float32
BF16 = jnp.bfloat16

D_MODEL = 2048
GRID_W = 64
NA_HEADS = 16
NA_WIDTH = 1024
NA_HEAD_DIM = 64
WIN_H = 8
WIN_W = 16
POOL_WINDOWS = (2, 4, 8, 16)
POOL_WIDTH = 1024
POOL_GROUP_DIM = 256
POOL_HALO = 16
N_MEM = 256
MEM_HEADS = 4
MEM_HEAD_DIM = 128
MEM_WIDTH = 512
N_EXPERT_GROUPS = 4
EXPERTS_PER_GROUP = 8
N_EXPERTS = 32
D_EXPERT = 512
MOE_BLOCK = 256
EPS = 1e-6
NEG = -1e30
LANES = 128
ROW_GROUP = 8
KV_CHUNK_ROWS = 4
SOFTMAX_ROWS = 16
VMEM_LIMIT = 56 * 1024 * 1024


def _rms(x, g):
    return x * lax.rsqrt(jnp.mean(x * x, axis=-1, keepdims=True) + EPS) * g


def _pack_bf16_pairs(x):
    n = x.shape[1] // 2
    bits = lax.bitcast_convert_type(x, jnp.uint32)
    return (bits[:, :n] >> 16) | (bits[:, n:] & jnp.uint32(0xFFFF0000))


def _unpack_bf16_pairs(u):
    lo = lax.bitcast_convert_type(u << 16, F32)
    hi = lax.bitcast_convert_type(u & jnp.uint32(0xFFFF0000), F32)
    return lo, hi


def _const_spec(shape):
    nd = len(shape)
    return pl.BlockSpec(shape, lambda *_: (0,) * nd, pipeline_mode=pl.Buffered(1))


def _params(*sem):
    return pltpu.CompilerParams(dimension_semantics=sem, vmem_limit_bytes=VMEM_LIMIT)


def _in_proj_kernel(x_ref, g_ref, w_ref, q_ref, k_ref, v_ref, u_ref):
    h = _rms(x_ref[...], g_ref[...]).astype(BF16)
    q_ref[...] = (jnp.dot(h, w_ref[:, 0:NA_WIDTH], preferred_element_type=F32)
                  * (NA_HEAD_DIM ** -0.5)).astype(BF16)
    k_ref[...] = jnp.dot(h, w_ref[:, NA_WIDTH:2 * NA_WIDTH], preferred_element_type=F32).astype(BF16)
    v_ref[...] = jnp.dot(h, w_ref[:, 2 * NA_WIDTH:3 * NA_WIDTH], preferred_element_type=F32).astype(BF16)
    u_ref[...] = jnp.dot(h, w_ref[:, 3 * NA_WIDTH:], preferred_element_type=F32)


def _in_proj(x2d, g_mix, w_in_bf, tm=512):
    t = x2d.shape[0]
    row = lambda i: (i, 0)
    return pl.pallas_call(
        _in_proj_kernel,
        grid=(t // tm,),
        in_specs=[pl.BlockSpec((tm, D_MODEL), row), _const_spec((1, D_MODEL)),
                  _const_spec(w_in_bf.shape)],
        out_specs=[pl.BlockSpec((tm, NA_WIDTH), row)] * 4,
        out_shape=[jax.ShapeDtypeStruct((t, NA_WIDTH), BF16)] * 3
        + [jax.ShapeDtypeStruct((t, POOL_WIDTH), F32)],
        compiler_params=_params("parallel"),
        name="in_proj",
    )(x2d, g_mix, w_in_bf)


def _norm_matmul_kernel(x_ref, g_ref, w_ref, o_ref):
    h = _rms(x_ref[...], g_ref[...]).astype(BF16)
    o_ref[...] = jnp.dot(h, w_ref[...], preferred_element_type=F32).astype(o_ref.dtype)


def _norm_matmul(x2d, g, w_bf, out_dtype, tm=256):
    t, k = x2d.shape
    n = w_bf.shape[1]
    return pl.pallas_call(
        _norm_matmul_kernel,
        grid=(t // tm,),
        in_specs=[pl.BlockSpec((tm, k), lambda i: (i, 0)), _const_spec((1, k)), _const_spec((k, n))],
        out_specs=pl.BlockSpec((tm, n), lambda i: (i, 0)),
        out_shape=jax.ShapeDtypeStruct((t, n), out_dtype),
        compiler_params=_params("parallel"),
        name="norm_matmul",
    )(x2d, g, w_bf)


def _pool_kernel(prev_ref, cur_ref, next_ref, w_ref, scale_ref, g_ref, o_ref, ext_ref, mix_ref, *, seq_len, tm):
    i = pl.program_id(1)
    n_i = pl.num_programs(1)
    ext_ref[0:POOL_HALO, :] = jnp.where(i > 0, prev_ref[0], 0.0)
    ext_ref[POOL_HALO:POOL_HALO + tm, :] = cur_ref[0]
    ext_ref[POOL_HALO + tm:, :] = jnp.where(i < n_i - 1, next_ref[0], 0.0)
    n_ext = tm + 2 * POOL_HALO
    pos = i * tm + lax.broadcasted_iota(jnp.int32, (tm, 1), 0)
    for gi, w in enumerate(POOL_WINDOWS):
        cols = slice(gi * POOL_GROUP_DIM, (gi + 1) * POOL_GROUP_DIM)
        e = ext_ref[:, cols]
        s = e + pltpu.roll(e, 1, 0)
        half = 1
        while 2 * half < w:
            s = pltpu.roll(s, half, 0) + pltpu.roll(s, n_ext - half, 0)
            half *= 2
        lo = jnp.maximum(pos - w // 2, 0)
        hi = jnp.minimum(pos - w // 2 + w, seq_len)
        cnt = (hi - lo).astype(F32)
        pooled = s[POOL_HALO:POOL_HALO + tm] / cnt - e[POOL_HALO:POOL_HALO + tm]
        mix_ref[:, cols] = jnp.dot(pooled.astype(BF16), w_ref[gi], preferred_element_type=F32)
    mixed = mix_ref[...] * scale_ref[...]
    o_ref[0] = _rms(mixed, g_ref[...]).astype(BF16)


def _pool_mixer(u3d, w_pool_bf, pool_scale, g_b, tm=512):
    b, seq_len, _ = u3d.shape
    tm = min(tm, seq_len)
    hb = tm // POOL_HALO
    n_hb = seq_len // POOL_HALO
    kern = functools.partial(_pool_kernel, seq_len=seq_len, tm=tm)
    return pl.pallas_call(
        kern,
        grid=(b, seq_len // tm),
        in_specs=[
            pl.BlockSpec((1, POOL_HALO, POOL_WIDTH), lambda bi, i: (bi, jnp.maximum(i * hb - 1, 0), 0)),
            pl.BlockSpec((1, tm, POOL_WIDTH), lambda bi, i: (bi, i, 0)),
            pl.BlockSpec((1, POOL_HALO, POOL_WIDTH), lambda bi, i: (bi, jnp.minimum((i + 1) * hb, n_hb - 1), 0)),
            _const_spec(w_pool_bf.shape), _const_spec((1, POOL_WIDTH)), _const_spec((1, POOL_WIDTH)),
        ],
        out_specs=pl.BlockSpec((1, tm, POOL_WIDTH), lambda bi, i: (bi, i, 0)),
        out_shape=jax.ShapeDtypeStruct((b, seq_len, POOL_WIDTH), BF16),
        scratch_shapes=[pltpu.VMEM((tm + 2 * POOL_HALO, POOL_WIDTH), F32), pltpu.VMEM((tm, POOL_WIDTH), F32)],
        compiler_params=_params("parallel", "parallel"),
        name="pool_mixer",
    )(u3d, u3d, u3d, w_pool_bf, pool_scale, g_b)


def _na_bias_table(rpb):
    c = jnp.arange(GRID_W)[:, None]
    kc = jnp.arange(GRID_W)[None, :]
    start = jnp.clip(c - WIN_W // 2, 0, GRID_W - WIN_W)
    valid = (kc >= start) & (kc < start + WIN_W)
    dc = kc - c + (WIN_W - 1)
    onehot = (dc[:, :, None] == jnp.arange(2 * WIN_W - 1)[None, None, :]).astype(F32)
    band = jnp.einsum("hrd,cqd->hrcq", rpb.astype(F32), onehot, precision=lax.Precision.HIGHEST)
    band = jnp.where(valid[None, None], band, NEG)
    n_dr = 2 * WIN_H - 2
    units = jnp.concatenate([band[:, :n_dr], band[:, 1:n_dr + 1]], axis=-1)
    units = units.reshape(NA_HEADS // 2, 2, n_dr, GRID_W, 2 * GRID_W)
    return jnp.transpose(units, (0, 2, 1, 3, 4)).reshape(NA_HEADS // 2, n_dr, 2 * GRID_W, 2 * GRID_W)


def _na_kernel(q_ref, k0, k1, k2, k3, v0, v1, v2, v3, bias_ref, g_ref, o_ref, kbuf, vbuf, acc, s_scr, p_scr, linv_scr,
               *, n_rows):
    g = pl.program_id(1)
    chunk = KV_CHUNK_ROWS * GRID_W
    for c, (kr, vr) in enumerate(((k0, v0), (k1, v1), (k2, v2), (k3, v3))):
        kbuf[c * chunk:(c + 1) * chunk, :] = kr[0]
        vbuf[c * chunk:(c + 1) * chunk, :] = vr[0]
    first_head = lax.broadcasted_iota(jnp.int32, (GRID_W, LANES), 1) < NA_HEAD_DIM
    n_keys = WIN_H * GRID_W

    n_pairs = NA_HEADS // 2

    def row_params(j):
        r = g * ROW_GROUP + j
        r0 = jnp.clip(r - WIN_H // 2, 0, n_rows - WIN_H)
        pat = r - r0
        koff = pl.multiple_of((r0 - (g * ROW_GROUP - KV_CHUNK_ROWS)) * GRID_W, GRID_W)
        qoff = pl.multiple_of(j * GRID_W, GRID_W)
        return pat, koff, qoff

    def scores(slot, prm):
        _, koff, qoff = prm
        for hp in range(n_pairs):
            cols = slice(hp * LANES, (hp + 1) * LANES)
            q2 = q_ref[0, pl.ds(qoff, GRID_W), cols]
            zero = jnp.zeros_like(q2)
            qq = jnp.concatenate([jnp.where(first_head, q2, zero), jnp.where(first_head, zero, q2)], axis=0)
            kk = kbuf[pl.ds(koff, n_keys), cols]
            s_scr[slot, hp] = lax.dot_general(qq, kk, (((1,), (1,)), ((), ())), preferred_element_type=F32)

    def softmax(slot, prm):
        pat = prm[0]
        for hp in range(n_pairs):
            for rc in range(2 * GRID_W // SOFTMAX_ROWS):
                rows = slice(rc * SOFTMAX_ROWS, (rc + 1) * SOFTMAX_ROWS)
                bias = jnp.concatenate([bias_ref[hp, 2 * u - pat + (WIN_H - 1), rows, :] for u in range(WIN_H // 2)],
                                       axis=-1)
                s = s_scr[slot, hp, rows, :] + bias
                p = jnp.exp(s - jnp.max(s, axis=-1, keepdims=True))
                p_scr[slot, hp, rows, :] = p.astype(BF16)
                linv = 1.0 / jnp.sum(p, axis=-1, keepdims=True)
                linv_scr[slot, hp, rows, :] = jnp.broadcast_to(linv, (SOFTMAX_ROWS, LANES))

    def outputs(slot, prm):
        _, koff, qoff = prm
        for hp in range(n_pairs):
            cols = slice(hp * LANES, (hp + 1) * LANES)
            vv = vbuf[pl.ds(koff, n_keys), cols]
            o = jnp.dot(p_scr[slot, hp], vv, preferred_element_type=F32) * linv_scr[slot, hp]
            acc[pl.ds(qoff, GRID_W), cols] = jnp.where(first_head, o[:GRID_W], o[GRID_W:])

    def row_pair_body(jj, carry):
        prms = [row_params(2 * jj + s) for s in range(2)]
        for phase in (scores, softmax, outputs):
            for s in range(2):
                phase(s, prms[s])
        return carry

    lax.fori_loop(0, ROW_GROUP // 2, row_pair_body, 0)
    o_ref[0] = _rms(acc[...], g_ref[...]).astype(BF16)


def _neighborhood_attention(q3d, k3d, v3d, bias_tbl, g_a):
    b, seq_len, _ = q3d.shape
    n_rows = seq_len // GRID_W
    tq = ROW_GROUP * GRID_W
    chunk = KV_CHUNK_ROWS * GRID_W
    n_chunks = seq_len // chunk

    def kv_spec(c):
        return pl.BlockSpec((1, chunk, NA_WIDTH),
                            lambda bi, g: (bi, jnp.clip(2 * g - 1 + c, 0, n_chunks - 1), 0))

    kern = functools.partial(_na_kernel, n_rows=n_rows)
    return pl.pallas_call(
        kern,
        grid=(b, n_rows // ROW_GROUP),
        in_specs=[pl.BlockSpec((1, tq, NA_WIDTH), lambda bi, g: (bi, g, 0))]
        + [kv_spec(c) for c in range(4)] * 2
        + [_const_spec(bias_tbl.shape), _const_spec((1, NA_WIDTH))],
        out_specs=pl.BlockSpec((1, tq, NA_WIDTH), lambda bi, g: (bi, g, 0)),
        out_shape=jax.ShapeDtypeStruct((b, seq_len, NA_WIDTH), BF16),
        scratch_shapes=[pltpu.VMEM((4 * chunk, NA_WIDTH), BF16), pltpu.VMEM((4 * chunk, NA_WIDTH), BF16),
                        pltpu.VMEM((tq, NA_WIDTH), F32),
                        pltpu.VMEM((2, NA_HEADS // 2, 2 * GRID_W, WIN_H * GRID_W), F32),
                        pltpu.VMEM((2, NA_HEADS // 2, 2 * GRID_W, WIN_H * GRID_W), BF16),
                        pltpu.VMEM((2, NA_HEADS // 2, 2 * GRID_W, LANES), F32)],
        compiler_params=_params("parallel", "parallel"),
        name="neighborhood_attention",
    )(q3d, k3d, k3d, k3d, k3d, v3d, v3d, v3d, v3d, bias_tbl, g_a)


def _post_kernel(x_ref, a_ref, p_ref, wout_ref, gc_ref, wq_ref, km_ref, vm_ref, wo_ref, gm_ref,
                 wr_ref, br_ref, x2_ref, h3p_ref, eid_ref, gate_ref):
    tm = x_ref.shape[1]
    x1 = (x_ref[0]
          + jnp.dot(a_ref[0], wout_ref[0:NA_WIDTH, :], preferred_element_type=F32)
          + jnp.dot(p_ref[0], wout_ref[NA_WIDTH:, :], preferred_element_type=F32))
    h2 = _rms(x1, gc_ref[...]).astype(BF16)
    qm = jnp.dot(h2, wq_ref[...], preferred_element_type=F32).astype(BF16)
    heads = []
    for hd in range(MEM_HEADS):
        cols = slice(hd * MEM_HEAD_DIM, (hd + 1) * MEM_HEAD_DIM)
        s = lax.dot_general(qm[:, cols], km_ref[0, :, cols], (((1,), (1,)), ((), ())),
                            preferred_element_type=F32) * (MEM_HEAD_DIM ** -0.5)
        m = jnp.max(s, axis=-1, keepdims=True)
        p = jnp.exp(s - m)
        l = jnp.sum(p, axis=-1, keepdims=True)
        o = jnp.dot(p.astype(BF16), vm_ref[0, :, cols], preferred_element_type=F32) / l
        heads.append(o.astype(BF16))
    o_all = jnp.concatenate(heads, axis=-1)
    x2 = x1 + jnp.dot(o_all, wo_ref[...], preferred_element_type=F32)
    x2_ref[0] = x2
    h3 = _rms(x2, gm_ref[...])

    h_bf = h3.astype(BF16)
    h3p_ref[0] = _pack_bf16_pairs(h_bf.astype(F32))
    logits = lax.dot_general(h_bf, wr_ref[...], (((1,), (1,)), ((), ())), preferred_element_type=F32) + br_ref[...]

    lane = lax.broadcasted_iota(jnp.int32, (tm, LANES), 1)
    lane_f = lane.astype(F32)
    big = float(LANES)
    cmask = lane < N_EXPERT_GROUPS
    lc = jnp.where(cmask, logits, NEG)
    cmax = jnp.max(lc, axis=-1, keepdims=True)
    g_idx = jnp.min(jnp.where(lc == cmax, lane_f, big), axis=-1, keepdims=True)
    g_gate = 1.0 / jnp.sum(jnp.exp(lc - cmax), axis=-1, keepdims=True)
    grp_f = ((lane - N_EXPERT_GROUPS) >> 3).astype(F32)
    in_fine = jnp.where(lane >= N_EXPERT_GROUPS, jnp.where(lane < N_EXPERT_GROUPS + N_EXPERTS, grp_f, -1.0), -1.0)
    lf = jnp.where(in_fine == g_idx, logits, NEG)
    v1 = jnp.max(lf, axis=-1, keepdims=True)
    i1 = jnp.min(jnp.where(lf == v1, lane_f, big), axis=-1, keepdims=True)
    lf2 = jnp.where(lane_f == i1, NEG, lf)
    v2 = jnp.max(lf2, axis=-1, keepdims=True)
    i2 = jnp.min(jnp.where(lf2 == v2, lane_f, big), axis=-1, keepdims=True)
    e21 = jnp.exp(v2 - v1)
    den = 1.0 / (1.0 + e21)
    two = lax.broadcasted_iota(jnp.int32, (tm, 2), 1)
    eid_ref[0] = jnp.where(two == 0, i1, i2).astype(jnp.int32) - N_EXPERT_GROUPS
    gate_ref[0] = jnp.where(two == 0, g_gate * den, g_gate * (e21 * den))


def _post_mix(x3d, a_n, p_n, w_out_bf, g_cross, w_q_bf, k_mem, v_mem, w_o_bf, g_moe, w_r_bf, b_r, tm=512):
    b, seq_len, _ = x3d.shape
    tok = lambda w: pl.BlockSpec((1, tm, w), lambda bi, i: (bi, i, 0))
    mem = pl.BlockSpec((1, N_MEM, MEM_WIDTH), lambda bi, i: (bi, 0, 0))
    return pl.pallas_call(
        _post_kernel,
        grid=(b, seq_len // tm),
        in_specs=[tok(D_MODEL), tok(NA_WIDTH), tok(POOL_WIDTH), _const_spec(w_out_bf.shape),
                  _const_spec((1, D_MODEL)), _const_spec(w_q_bf.shape), mem, mem, _const_spec(w_o_bf.shape),
                  _const_spec((1, D_MODEL)), _const_spec(w_r_bf.shape),
                  _const_spec((1, LANES))],
        out_specs=[tok(D_MODEL), tok(D_MODEL // 2), tok(2), tok(2)],
        out_shape=[jax.ShapeDtypeStruct((b, seq_len, D_MODEL), F32),
                   jax.ShapeDtypeStruct((b, seq_len, D_MODEL // 2), jnp.uint32),
                   jax.ShapeDtypeStruct((b, seq_len, 2), jnp.int32), jax.ShapeDtypeStruct((b, seq_len, 2), F32)],
        compiler_params=_params("parallel", "parallel"),
        name="post_mix",
    )(x3d, a_n, p_n, w_out_bf, g_cross, w_q_bf, k_mem, v_mem, w_o_bf, g_moe, w_r_bf, b_r)


def _rank_kernel(eid_ref, rank_ref, cnt_ref, carry):
    tm = eid_ref.shape[0]

    @pl.when(pl.program_id(0) == 0)
    def _():
        carry[...] = jnp.zeros_like(carry)

    lane = lax.broadcasted_iota(jnp.int32, (tm, LANES), 1)
    e1 = eid_ref[:, 0:1]
    e2 = eid_ref[:, 1:2]
    hit1 = lane == e1
    hit2 = lane == e2
    onehot = jnp.where(hit1, 1.0, 0.0) + jnp.where(hit2, 1.0, 0.0)
    row = lax.broadcasted_iota(jnp.int32, (tm, tm), 0)
    col = lax.broadcasted_iota(jnp.int32, (tm, tm), 1)
    earlier = jnp.where(col < row, 1.0, 0.0).astype(BF16)
    before = jnp.dot(earlier, onehot.astype(BF16), preferred_element_type=F32) + carry[...]
    r1 = jnp.sum(jnp.where(hit1, before, 0.0), axis=-1, keepdims=True)
    r2 = jnp.sum(jnp.where(hit2, before, 0.0), axis=-1, keepdims=True)
    two = lax.broadcasted_iota(jnp.int32, (tm, 2), 1)
    rank_ref[...] = jnp.where(two == 0, r1, r2).astype(jnp.int32)
    total = carry[...] + jnp.sum(onehot, axis=0, keepdims=True)
    carry[...] = total
    cnt_ref[...] = total.astype(jnp.int32)


def _expert_ranks(eid2d, tm=512):
    t = eid2d.shape[0]
    return pl.pallas_call(
        _rank_kernel,
        grid=(t // tm,),
        in_specs=[pl.BlockSpec((tm, 2), lambda i: (i, 0))],
        out_specs=[pl.BlockSpec((tm, 2), lambda i: (i, 0)), pl.BlockSpec((1, LANES), lambda i: (0, 0))],
        out_shape=[jax.ShapeDtypeStruct((t, 2), jnp.int32), jax.ShapeDtypeStruct((1, LANES), jnp.int32)],
        scratch_shapes=[pltpu.VMEM((1, LANES), F32)],
        compiler_params=_params("arbitrary"),
        name="expert_ranks",
    )(eid2d)


def _dispatch_kernel(dest_ref, pad_lo_ref, pad_hi_ref, h_ref, xs_ref, zrow, sem, zsem, *, tm, n_tok):
    i = pl.program_id(0)
    base = i * tm
    unroll = 8

    def row_copy(r, k):
        return pltpu.make_async_copy(h_ref.at[pl.ds(r, 1)],
                                     xs_ref.at[pl.ds(dest_ref[k * n_tok + base + r], 1)], sem)

    def issue(c, carry):
        for u in range(unroll):
            for k in range(2):
                row_copy(c * unroll + u, k).start(priority=k)
        return carry

    lax.fori_loop(0, tm // unroll, issue, 0)

    @pl.when(i == 0)
    def _():
        zrow[...] = jnp.zeros_like(zrow)

        def zero_copies(wait):
            def chunk(pos, size):
                cp = pltpu.make_async_copy(zrow.at[pl.ds(0, size)], xs_ref.at[pl.ds(pos, size)], zsem)
                cp.wait() if wait else cp.start()

            def segment(e, carry):
                lo = pad_lo_ref[e]
                length = pad_hi_ref[e] - lo
                head = length & 7

                def single(r, c2):
                    chunk(lo + r, 1)
                    return c2

                lax.fori_loop(0, head, single, 0)
                pos = lo + head
                size = MOE_BLOCK // 2
                while size >= 8:
                    bit = length & size
                    pl.when(bit != 0)(functools.partial(chunk, pl.multiple_of(pos, 8), size))
                    pos = pos + bit
                    size //= 2
                return carry

            lax.fori_loop(0, N_EXPERTS, segment, 0)

            def tail(c, carry):
                chunk(pl.multiple_of(pad_lo_ref[N_EXPERTS] + c * MOE_BLOCK, MOE_BLOCK), MOE_BLOCK)
                return carry

            lax.fori_loop(0, (pad_hi_ref[N_EXPERTS] - pad_lo_ref[N_EXPERTS]) // MOE_BLOCK, tail, 0)

        zero_copies(wait=False)
        zero_copies(wait=True)

    def drain(c, carry):
        for u in range(unroll):
            for k in range(2):
                row_copy(c * unroll + u, k).wait()
        return carry

    lax.fori_loop(0, tm // unroll, drain, 0)


def _dispatch(dest_kmajor, pad_lo, pad_hi, h3p_2d, n_slots, tm=512):
    t, half = h3p_2d.shape
    kern = functools.partial(_dispatch_kernel, tm=tm, n_tok=t)
    return pl.pallas_call(
        kern,
        grid_spec=pltpu.PrefetchScalarGridSpec(
            num_scalar_prefetch=3,
            grid=(t // tm,),
            in_specs=[pl.BlockSpec((tm, half), lambda i, *_: (i, 0))],
            out_specs=pl.BlockSpec(memory_space=pl.ANY),
            scratch_shapes=[pltpu.VMEM((MOE_BLOCK, half), jnp.uint32), pltpu.SemaphoreType.DMA,
                            pltpu.SemaphoreType.DMA],
        ),
        out_shape=jax.ShapeDtypeStruct((n_slots, half), jnp.uint32),
        compiler_params=pltpu.CompilerParams(dimension_semantics=("arbitrary",), has_side_effects=True),
        name="moe_dispatch",
    )(dest_kmajor, pad_lo, pad_hi, h3p_2d)


def _row_gather(src_ref, idx_ref, idx_base, n_rows, dst_ref, sem):
    for r in range(n_rows):
        pltpu.make_async_copy(src_ref.at[pl.ds(idx_ref[idx_base + r], 1)], dst_ref.at[pl.ds(r, 1)], sem).start()


def _row_gather_wait(src_ref, n_rows, dst_ref, sem):
    for r in range(n_rows):
        pltpu.make_async_copy(src_ref.at[pl.ds(0, 1)], dst_ref.at[pl.ds(r, 1)], sem).wait()


def _expert_kernel(be_ref, nused_ref, xs_ref, wgu_ref, wd_ref, ys_ref):
    del be_ref
    i = pl.program_id(0)

    @pl.when(i < nused_ref[0])
    def _():
        x_lo, x_hi = _unpack_bf16_pairs(xs_ref[...])
        half = x_lo.shape[1]
        gu = (jnp.dot(x_lo.astype(BF16), wgu_ref[0, :half, :], preferred_element_type=F32)
              + jnp.dot(x_hi.astype(BF16), wgu_ref[0, half:, :], preferred_element_type=F32))
        gate = gu[:, :D_EXPERT]
        hid = gate * jax.nn.sigmoid(gate) * gu[:, D_EXPERT:]
        y = jnp.dot(hid.astype(BF16), wd_ref[0], preferred_element_type=F32)
        ys_ref[...] = _pack_bf16_pairs(y.astype(BF16).astype(F32))

    @pl.when(i >= nused_ref[0])
    def _():
        ys_ref[...] = jnp.zeros_like(ys_ref)


def _experts(block_expert, n_used, xs, w_gu_bf, w_d_bf):
    n_slots, half = xs.shape
    d = 2 * half
    return pl.pallas_call(
        _expert_kernel,
        grid_spec=pltpu.PrefetchScalarGridSpec(
            num_scalar_prefetch=2,
            grid=(n_slots // MOE_BLOCK,),
            in_specs=[pl.BlockSpec((MOE_BLOCK, half), lambda i, be, nu: (i, 0)),
                      pl.BlockSpec((1, d, 2 * D_EXPERT), lambda i, be, nu: (be[i], 0, 0)),
                      pl.BlockSpec((1, D_EXPERT, d), lambda i, be, nu: (be[i], 0, 0))],
            out_specs=pl.BlockSpec((MOE_BLOCK, half), lambda i, be, nu: (i, 0)),
        ),
        out_shape=jax.ShapeDtypeStruct((n_slots, half), jnp.uint32),
        compiler_params=_params("arbitrary"),
        name="moe_experts",
    )(block_expert, n_used, xs, w_gu_bf, w_d_bf)


def _combine_kernel(dest_ref, x2_ref, gate_ref, g_ref, ys_ref, o_ref, ybuf, sems, *, tm):
    i = pl.program_id(0)
    n = pl.num_programs(0)
    slot = i % 2
    unroll = 8

    n_tok = n * tm

    def start_tile(tile, buf_slot):
        def issue(c, carry):
            for k in range(2):
                _row_gather(ys_ref, dest_ref, k * n_tok + tile * tm + c * unroll, unroll,
                            ybuf.at[buf_slot, k, pl.ds(c * unroll, unroll)], sems.at[buf_slot])
            return carry

        lax.fori_loop(0, tm // unroll, issue, 0)

    @pl.when(i == 0)
    def _():
        start_tile(0, 0)

    @pl.when(i + 1 < n)
    def _():
        start_tile(i + 1, 1 - slot)

    def drain(c, carry):
        for k in range(2):
            _row_gather_wait(ys_ref, unroll, ybuf.at[slot, k, pl.ds(c * unroll, unroll)], sems.at[slot])
        return carry

    lax.fori_loop(0, tm // unroll, drain, 0)
    gates = gate_ref[...]
    lo0, hi0 = _unpack_bf16_pairs(ybuf[slot, 0])
    lo1, hi1 = _unpack_bf16_pairs(ybuf[slot, 1])
    y = jnp.concatenate([gates[:, 0:1] * lo0 + gates[:, 1:2] * lo1,
                         gates[:, 0:1] * hi0 + gates[:, 1:2] * hi1], axis=-1)
    o_ref[...] = _rms(x2_ref[...] + y, g_ref[...])


def _combine(dest_flat, x2_2d, gate2d, g_final, ys, tm=256):
    t, d = x2_2d.shape
    kern = functools.partial(_combine_kernel, tm=tm)
    return pl.pallas_call(
        kern,
        grid_spec=pltpu.PrefetchScalarGridSpec(
            num_scalar_prefetch=1,
            grid=(t // tm,),
            in_specs=[pl.BlockSpec((tm, d), lambda i, dst: (i, 0)),
                      pl.BlockSpec((tm, 2), lambda i, dst: (i, 0)),
                      pl.BlockSpec((1, d), lambda i, dst: (0, 0)),
                      pl.BlockSpec(memory_space=pl.ANY)],
            out_specs=pl.BlockSpec((tm, d), lambda i, dst: (i, 0)),
            scratch_shapes=[pltpu.VMEM((2, 2, tm, d // 2), jnp.uint32), pltpu.SemaphoreType.DMA((2,))],
        ),
        out_shape=jax.ShapeDtypeStruct((t, d), F32),
        compiler_params=_params("arbitrary"),
        name="moe_combine",
    )(dest_flat, x2_2d, gate2d, g_final, ys)


def _encode(x, mem, p):
    b, seq_len, d = x.shape
    t = b * seq_len
    q, k, v, u = _in_proj(x.reshape(t, d), p["g_mix"], p["w_in"])
    shape3 = (b, seq_len, NA_WIDTH)
    a_n = _neighborhood_attention(q.reshape(shape3), k.reshape(shape3), v.reshape(shape3), p["na_bias"], p["g_branch_a"])
    p_n = _pool_mixer(u.reshape(shape3), p["w_pool"], p["pool_scale"], p["g_branch_b"])
    kv = _norm_matmul(mem.reshape(b * N_MEM, d), p["g_mem"], p["w_kv"], BF16).reshape(b, N_MEM, 2 * MEM_WIDTH)
    x2, h3p, eid, gate = _post_mix(x, a_n, p_n, p["w_out"], p["g_cross"], p["w_q"], kv[:, :, :MEM_WIDTH],
                                   kv[:, :, MEM_WIDTH:], p["w_o"], p["g_moe"], p["w_r"], p["b_r"])
    x2 = x2.reshape(t, d)
    eid2d = eid.reshape(t, 2)
    rank, counts = _expert_ranks(eid2d)
    counts = counts[0, :N_EXPERTS]
    padded = ((counts + MOE_BLOCK - 1) // MOE_BLOCK) * MOE_BLOCK
    padded_ends = jnp.cumsum(padded)
    padded_starts = padded_ends - padded
    n_blocks = -(-(2 * t + N_EXPERTS * (MOE_BLOCK - 1)) // MOE_BLOCK)
    n_slots = n_blocks * MOE_BLOCK
    block_pos = jnp.arange(n_blocks, dtype=jnp.int32) * MOE_BLOCK
    block_expert = jnp.minimum(jnp.sum(padded_ends[None, :] <= block_pos[:, None], axis=1), N_EXPERTS - 1).astype(jnp.int32)
    is_e = eid2d[:, :, None] == jnp.arange(N_EXPERTS, dtype=jnp.int32)
    dest = (jnp.sum(jnp.where(is_e, padded_starts, 0), axis=-1) + rank).astype(jnp.int32)
    dest_kmajor = dest.T.reshape(-1)
    pad_lo = jnp.concatenate([padded_starts + counts, padded_ends[-1:]]).astype(jnp.int32)
    pad_hi = jnp.concatenate([padded_ends, jnp.full((1,), n_slots, jnp.int32)]).astype(jnp.int32)
    n_used = (padded_ends[-1:] // MOE_BLOCK).astype(jnp.int32)
    xs = _dispatch(dest_kmajor, pad_lo, pad_hi, h3p.reshape(t, d // 2), n_slots)
    ys = _experts(block_expert, n_used, xs, p["w_gu"], p["w_d"])
    y = _combine(dest_kmajor, x2, gate.reshape(t, 2), p["g_final"], ys)
    return y.reshape(b, seq_len, d)


def kernel(x_prompt, x_sample, mem_prompt, mem_sample, g_mix, w_in, rpb, w_pool, pool_scale, g_branch_a, g_branch_b, w_out, g_cross, g_mem, w_q_mem, w_kv_mem, w_o_mem, g_moe, w_coarse, b_coarse, w_fine, b_fine, w_gate_e, w_up_e, w_down_e, g_final):
    depth = w_in.shape[0]
    xp, xs = x_prompt, x_sample
    row = lambda a: a.reshape(1, -1).astype(F32)
    for l in range(depth):
        w_r = jnp.concatenate([w_coarse[l].T, w_fine[l].T], axis=0).astype(F32)
        w_r = jnp.pad(w_r, ((0, LANES - w_r.shape[0]), (0, 0)))
        b_r = jnp.concatenate([b_coarse[l], b_fine[l]]).astype(F32)
        p = {
            "g_mix": row(g_mix[l]), "w_in": w_in[l].astype(BF16),
            "na_bias": _na_bias_table(rpb[l]), "g_branch_a": row(g_branch_a[l]),
            "w_pool": w_pool[l].astype(BF16), "pool_scale": row(pool_scale[l]), "g_branch_b": row(g_branch_b[l]),
            "w_out": w_out[l].astype(BF16), "g_cross": row(g_cross[l]), "g_mem": row(g_mem[l]),
            "w_q": w_q_mem[l].astype(BF16), "w_kv": w_kv_mem[l].astype(BF16), "w_o": w_o_mem[l].astype(BF16),
            "g_moe": row(g_moe[l]),
            "w_r": w_r.astype(BF16),
            "b_r": jnp.pad(b_r, (0, LANES - b_r.shape[0])).reshape(1, LANES),
            "w_gu": jnp.concatenate([w_gate_e[l], w_up_e[l]], axis=-1).astype(BF16),
            "w_d": w_down_e[l].astype(BF16),
            "g_final": row(g_final),
        }
        assert depth == 1, "final-norm fusion assumes a single layer"
        xp = _encode(xp, mem_prompt, p)
        xs = _encode(xs, mem_sample, p)
    return (xp, xs)
```
